```python
import math
import jax, jax.numpy as jnp
from jax import lax
import numpy as np

D_MODEL = 1024
BATCH = 8
SEQ = 4096
DEPTH = 2

HEAD_DIM = 64
N_HEADS_DIL = 8
N_HEADS_WIN = 8
N_KV_WIN = 2
DIL_PATTERNS = ((128, 1), (512, 4), (2048, 16))
WIN_SIDE = 128
WIN_BLOCK = 128
D_DIL = N_HEADS_DIL * HEAD_DIM
D_WIN = N_HEADS_WIN * HEAD_DIM
D_KV_WIN = N_KV_WIN * HEAD_DIM
MIX_WIDTH = D_DIL + D_WIN
IN_COLS = 3 * D_DIL + D_WIN + 2 * D_KV_WIN
D_FF = ((8 * D_MODEL + 3 * 256 - 1) // (3 * 256)) * 256
EPS = 1e-6
NEG = -1e30
N_ALIBI = N_HEADS_WIN + N_HEADS_DIL

kernel_name = "hybrid_dilated_window_sink_encoder"


def rmsnorm(x, g):
    xf = x.astype(jnp.float32)
    y = xf * lax.rsqrt(jnp.mean(xf * xf, axis=-1, keepdims=True) + EPS)
    return (y * g.astype(jnp.float32)).astype(x.dtype)


def alibi_slopes(n):
    return np.array([2.0 ** (-8.0 * (i + 1) / n) for i in range(n)], dtype=np.float32)


def band_partial(q, k, v, side, blk, slopes, dist_scale):
    B, L, Hq, Dh = q.shape
    Hk = k.shape[2]
    grp = Hq // Hk
    n_blk = -(-L // blk)
    Lp = n_blk * blk
    pad = Lp - L
    qp = jnp.pad(q, ((0, 0), (0, pad), (0, 0), (0, 0))).reshape(B, n_blk, blk, Hk, grp, Dh)
    kp = jnp.pad(k, ((0, 0), (blk, pad + blk), (0, 0), (0, 0))).reshape(B, n_blk + 2, blk, Hk, Dh)
    vp = jnp.pad(v, ((0, 0), (blk, pad + blk), (0, 0), (0, 0))).reshape(B, n_blk + 2, blk, Hk, Dh)
    kw = jnp.concatenate([kp[:, 0:n_blk], kp[:, 1:n_blk + 1], kp[:, 2:n_blk + 2]], axis=2)
    vw = jnp.concatenate([vp[:, 0:n_blk], vp[:, 1:n_blk + 1], vp[:, 2:n_blk + 2]], axis=2)
    i_idx = np.arange(blk)[:, None]
    j_idx = np.arange(3 * blk)[None, :]
    rel = j_idx - blk - i_idx
    kpos = np.arange(n_blk)[:, None] * blk - blk + np.arange(3 * blk)[None, :]
    valid = (np.abs(rel) <= side)[None] & ((kpos >= 0) & (kpos < L))[:, None, :]
    dist = jnp.asarray(np.abs(rel).astype(np.float32) * dist_scale)
    sl = slopes.astype(jnp.float32).reshape(Hk, grp)
    s = jnp.einsum('bnqhgd,bnkhd->bhgnqk', qp.astype(jnp.float32), kw.astype(jnp.float32)) * (HEAD_DIM ** -0.5)
    s = s - sl[:, :, None, None, None] * dist[None, None, None]
    s = jnp.where(jnp.asarray(valid)[None, None, None], s, NEG)
    m = jnp.max(s, axis=-1)
    p = jnp.exp(s - m[..., None])
    l = jnp.sum(p, axis=-1)
    o = jnp.einsum('bhgnqk,bnkhd->bnqhgd', p, vw.astype(jnp.float32))
    m = m.transpose(0, 3, 4, 1, 2).reshape(B, Lp, Hq)[:, :L]
    l = l.transpose(0, 3, 4, 1, 2).reshape(B, Lp, Hq)[:, :L]
    o = o.reshape(B, Lp, Hq, Dh)[:, :L]
    return m, l, o


def dilated_attention(q, k, v, slopes):
    B, S, H, Dh = q.shape
    ms, ls, os_ = [], [], []
    for window, dil in DIL_PATTERNS:
        side = window // (2 * dil)
        L = S // dil

        def to_res(t):
            return t.reshape(B, L, dil, H, Dh).transpose(0, 2, 1, 3, 4).reshape(B * dil, L, H, Dh)

        m, l, o = band_partial(to_res(q), to_res(k), to_res(v), side, side, slopes, float(dil))
        ms.append(m.reshape(B, dil, L, H).transpose(0, 2, 1, 3).reshape(B, S, H))
        ls.append(l.reshape(B, dil, L, H).transpose(0, 2, 1, 3).reshape(B, S, H))
        os_.append(o.reshape(B, dil, L, H, Dh).transpose(0, 2, 1, 3, 4).reshape(B, S, H, Dh))
    m_all = jnp.stack(ms)
    l_all = jnp.stack(ls)
    o_all = jnp.stack(os_)
    w = jnp.exp(m_all - jnp.max(m_all, axis=0, keepdims=True))
    num = jnp.sum(w[..., None] * o_all, axis=0)
    den = jnp.sum(w * l_all, axis=0)
    return num / den[..., None]


def window_gqa_sink(q, k, v, slopes, sink):
    m, l, o = band_partial(q, k, v, WIN_SIDE, WIN_BLOCK, slopes, 1.0)
    sk = sink.astype(jnp.float32)
    M = jnp.maximum(m, sk)
    a = jnp.exp(m - M)
    den = l * a + jnp.exp(sk - M)
    return o * (a / den)[..., None]


def setup_inputs(seed: int = 0) -> dict:
    key = jax.random.key(seed)
    ks = jax.random.split(key, 12)
    f32 = jnp.float32
    x = jax.random.normal(ks[0], (BATCH, SEQ, D_MODEL), f32)
    g_mix = 1.0 + 0.02 * jax.random.normal(ks[1], (DEPTH, D_MODEL), f32)
    w_in = jax.random.normal(ks[2], (DEPTH, D_MODEL, IN_COLS), f32) * D_MODEL ** -0.5
    g_out_dil = 1.0 + 0.02 * jax.random.normal(ks[3], (DEPTH, D_DIL), f32)
    g_out_win = 1.0 + 0.02 * jax.random.normal(ks[4], (DEPTH, D_WIN), f32)
    sink = 0.1 * jax.random.normal(ks[5], (DEPTH, N_HEADS_WIN), f32)
    w_out = jax.random.normal(ks[6], (DEPTH, MIX_WIDTH, D_MODEL), f32) * MIX_WIDTH ** -0.5
    g_ffn = 1.0 + 0.02 * jax.random.normal(ks[7], (DEPTH, D_MODEL), f32)
    w_gate = jax.random.normal(ks[8], (DEPTH, D_MODEL, D_FF), f32) * D_MODEL ** -0.5
    w_up = jax.random.normal(ks[9], (DEPTH, D_MODEL, D_FF), f32) * D_MODEL ** -0.5
    w_down = jax.random.normal(ks[10], (DEPTH, D_FF, D_MODEL), f32) * D_FF ** -0.5
    g_final = 1.0 + 0.02 * jax.random.normal(ks[11], (D_MODEL,), f32)
    return {"x": x, "g_mix": g_mix, "w_in": w_in, "g_out_dil": g_out_dil, "g_out_win": g_out_win,
            "sink": sink, "w_out": w_out, "g_ffn": g_ffn, "w_gate": w_gate, "w_up": w_up,
            "w_down": w_down, "g_final": g_final}


def reference(x, g_mix, w_in, g_out_dil, g_out_win, sink, w_out, g_ffn, w_gate, w_up, w_down, g_final):
    B, S, _ = x.shape
    slopes = jnp.asarray(alibi_slopes(N_ALIBI))
    slopes_win = slopes[:N_HEADS_WIN]
    slopes_dil = slopes[N_HEADS_WIN:]
    o1 = D_DIL
    o2 = 2 * D_DIL
    o3 = 3 * D_DIL
    o4 = o3 + D_WIN
    o5 = o4 + D_KV_WIN
    for i in range(DEPTH):
        h = rmsnorm(x, g_mix[i])
        proj = h @ w_in[i]
        qa = proj[..., :o1].reshape(B, S, N_HEADS_DIL, HEAD_DIM)
        ka = proj[..., o1:o2].reshape(B, S, N_HEADS_DIL, HEAD_DIM)
        va = proj[..., o2:o3].reshape(B, S, N_HEADS_DIL, HEAD_DIM)
        qb = proj[..., o3:o4].reshape(B, S, N_HEADS_WIN, HEAD_DIM)
        kb = proj[..., o4:o5].reshape(B, S, N_KV_WIN, HEAD_DIM)
        vb = proj[..., o5:].reshape(B, S, N_KV_WIN, HEAD_DIM)
        ya = dilated_attention(qa, ka, va, slopes_dil).reshape(B, S, D_DIL).astype(x.dtype)
        yb = window_gqa_sink(qb, kb, vb, slopes_win, sink[i]).reshape(B, S, D_WIN).astype(x.dtype)
        y = jnp.concatenate([rmsnorm(ya, g_out_dil[i]), rmsnorm(yb, g_out_win[i])], axis=-1)
        x = x + y @ w_out[i]
        h = rmsnorm(x, g_ffn[i])
        x = x + (jax.nn.silu(h @ w_gate[i]) * (h @ w_up[i])) @ w_down[i]
    return rmsnorm(x, g_final)
```

```python
import functools

import jax
import jax.numpy as jnp
import numpy as np
from jax import lax
from jax.experimental import pallas as pl
from jax.experimental.pallas import tpu as pltpu

D_MODEL = 1024
HEAD_DIM = 64
N_HEADS = 8
N_KV_WIN = 2
DIL_PATTERNS = ((128, 1), (512, 4), (2048, 16))
WIN_SIDE = 128
D_MIX = N_HEADS * HEAD_DIM
D_KV_WIN = N_KV_WIN * HEAD_DIM
D_FF = 2816
EPS = 1e-6
NEG = -1e30
N_ALIBI = 2 * N_HEADS

LANES = 128
FF_CHUNK = 256
Q_BLOCK = 128
Q_TILE = 1024
TOK_TILE_PROJ = 512
TOK_TILE_FFN = 512
VMEM_LIMIT = 56 * 1024 * 1024

WIN_HEAD_ORDER = (0, 4, 1, 5, 2, 6, 3, 7)

_F32 = jnp.float32
_BF16 = jnp.bfloat16


def _alibi_slopes(n):
    return np.array([2.0 ** (-8.0 * (i + 1) / n) for i in range(n)], dtype=np.float32)


def _bias_table(side, dil, slopes):
    kw = Q_BLOCK + 2 * side
    i = np.arange(Q_BLOCK)[:, None]
    j = np.arange(kw)[None, :]
    out = np.empty((3, len(slopes), Q_BLOCK, kw), np.float32)
    for v in range(3):
        rel = np.abs(j - i - side * v)
        dist = rel.astype(np.float32) * np.float32(dil)
        for h, sl in enumerate(slopes):
            out[v, h] = np.where(rel <= side, -(np.float32(sl) * dist), np.float32(NEG))
    return out


def _rmsnorm(x, g):
    return x * lax.rsqrt(jnp.mean(x * x, axis=-1, keepdims=True) + EPS) * g


def _inproj_kernel(x_ref, g_ref, w_ref, qa_ref, ka_ref, va_ref, qb_ref, kb_ref, vb_ref):
    h = _rmsnorm(x_ref[...], g_ref[...]).astype(_BF16)
    scale = HEAD_DIM ** -0.5
    col = 0
    for ref, width, mul in ((qa_ref, D_MIX, scale), (ka_ref, D_MIX, None), (va_ref, D_MIX, None),
                            (qb_ref, D_MIX, scale), (kb_ref, D_KV_WIN, None), (vb_ref, D_KV_WIN, None)):
        p = jnp.dot(h, w_ref[:, col:col + width], preferred_element_type=_F32)
        if mul is not None:
            p = p * mul
        ref[...] = p.astype(_BF16)
        col += width


def _inproj(x2, g, w):
    n = x2.shape[0]
    tm = TOK_TILE_PROJ
    widths = (D_MIX, D_MIX, D_MIX, D_MIX, D_KV_WIN, D_KV_WIN)
    return pl.pallas_call(
        _inproj_kernel,
        grid=(n // tm,),
        in_specs=[pl.BlockSpec((tm, D_MODEL), lambda i: (i, 0)),
                  pl.BlockSpec((1, D_MODEL), lambda i: (0, 0)),
                  pl.BlockSpec(w.shape, lambda i: (0, 0))],
        out_specs=[pl.BlockSpec((tm, wd), lambda i: (i, 0)) for wd in widths],
        out_shape=[jax.ShapeDtypeStruct((n, wd), _BF16) for wd in widths],
        compiler_params=pltpu.CompilerParams(dimension_semantics=("arbitrary",),
                                             vmem_limit_bytes=VMEM_LIMIT),
        name="inproj",
    )(x2, g, w)


def _block_window(t, i, side, length):
    kw = Q_BLOCK + 2 * side
    q0 = t * Q_TILE + i * Q_BLOCK
    kstart = jnp.clip(q0 - side, 0, length - kw)
    var = lax.shift_right_logical(q0 - kstart, side.bit_length() - 1)
    return pl.multiple_of(i * Q_BLOCK, Q_BLOCK), pl.multiple_of(kstart, side), var


def _dil_attn_kernel(q_ref, k_ref, v_ref, bias_ref, o_ref, st_ref, *, side, length):
    t = pl.program_id(2)
    kw = Q_BLOCK + 2 * side
    lane = lax.broadcasted_iota(jnp.int32, (Q_BLOCK, LANES), 1)
    lo = lane < HEAD_DIM

    def body(i, carry):
        r0, kstart, var = _block_window(t, i, side, length)
        st = jnp.zeros((Q_BLOCK, LANES), _F32)
        for pair in range(N_HEADS // 2):
            cols = slice(pair * LANES, (pair + 1) * LANES)
            qp = q_ref[pl.ds(r0, Q_BLOCK), cols]
            kp = k_ref[pl.ds(kstart, kw), cols]
            vp = v_ref[pl.ds(kstart, kw), cols]
            halves = []
            for half in range(2):
                h = 2 * pair + half
                qm = jnp.where(lo if half == 0 else jnp.logical_not(lo), qp, jnp.zeros_like(qp))
                s = lax.dot_general(qm, kp, (((1,), (1,)), ((), ())), preferred_element_type=_F32)
                s = s + bias_ref[var, h]
                m = jnp.max(s, axis=-1, keepdims=True)
                p = jnp.exp(s - m)
                l = jnp.sum(p, axis=-1, keepdims=True)
                o = jnp.dot(p.astype(_BF16), vp, preferred_element_type=_F32)
                halves.append(o * (1.0 / l))
                st = jnp.where(lane == h, m, st)
                st = jnp.where(lane == N_HEADS + h, l, st)
            o_ref[pl.ds(r0, Q_BLOCK), cols] = jnp.where(lo, halves[0], halves[1]).astype(_BF16)
        st_ref[pl.ds(r0, Q_BLOCK), :] = st
        return carry

    lax.fori_loop(0, min(Q_TILE, length) // Q_BLOCK, body, 0)


def _dil_attention(q, k, v, bias, batch, seq, window, dil):
    side = window // (2 * dil)
    length = seq // dil
    tq = min(Q_TILE, length)
    view = lambda a, c: a.reshape(batch, length, dil * c)
    qkv_tile = pl.BlockSpec((None, tq, D_MIX), lambda b, r, t: (b, t, r))
    kv_full = pl.BlockSpec((None, length, D_MIX), lambda b, r, t: (b, 0, r))
    o, st = pl.pallas_call(
        functools.partial(_dil_attn_kernel, side=side, length=length),
        grid=(batch, dil, length // tq),
        in_specs=[qkv_tile, kv_full, kv_full,
                  pl.BlockSpec(bias.shape, lambda b, r, t: (0, 0, 0, 0))],
        out_specs=[qkv_tile, pl.BlockSpec((None, tq, LANES), lambda b, r, t: (b, t, r))],
        out_shape=[jax.ShapeDtypeStruct((batch, length, dil * D_MIX), _BF16),
                   jax.ShapeDtypeStruct((batch, length, dil * LANES), _F32)],
        compiler_params=pltpu.CompilerParams(dimension_semantics=("arbitrary",) * 3,
                                             vmem_limit_bytes=VMEM_LIMIT),
        name=f"dil_attn_d{dil}",
    )(view(q, D_MIX), view(k, D_MIX), view(v, D_MIX), bias)
    return o.reshape(batch * seq, D_MIX), st.reshape(batch * seq, LANES)


def _win_attn_kernel(sink_ref, q_ref, k_ref, v_ref, bias_ref, o_ref, *, length):
    t = pl.program_id(1)
    side = WIN_SIDE
    kw = Q_BLOCK + 2 * side
    lane = lax.broadcasted_iota(jnp.int32, (Q_BLOCK, LANES), 1)
    lo = lane < HEAD_DIM

    def body(i, carry):
        r0, kstart, var = _block_window(t, i, side, length)
        kp = k_ref[pl.ds(kstart, kw), :]
        vp = v_ref[pl.ds(kstart, kw), :]
        for pair in range(N_HEADS // 2):
            cols = slice(pair * LANES, (pair + 1) * LANES)
            qp = q_ref[pl.ds(r0, Q_BLOCK), cols]
            halves = []
            for half in range(2):
                h = 2 * pair + half
                qm = jnp.where(lo if half == 0 else jnp.logical_not(lo), qp, jnp.zeros_like(qp))
                s = lax.dot_general(qm, kp, (((1,), (1,)), ((), ())), preferred_element_type=_F32)
                s = s + bias_ref[var, h]
                m = jnp.max(s, axis=-1, keepdims=True)
                p = jnp.exp(s - m)
                l = jnp.sum(p, axis=-1, keepdims=True)
                o = jnp.dot(p.astype(_BF16), vp, preferred_element_type=_F32)
                sk = sink_ref[h]
                big = jnp.maximum(m, sk)
                a = jnp.exp(m - big)
                den = l * a + jnp.exp(sk - big)
                halves.append(o * (a / den))
            o_ref[pl.ds(r0, Q_BLOCK), cols] = jnp.where(lo, halves[0], halves[1]).astype(_BF16)
        return carry

    lax.fori_loop(0, Q_TILE // Q_BLOCK, body, 0)


def _win_attention(q, k, v, bias, sink, batch, seq):
    view = lambda a: a.reshape(batch, seq, a.shape[-1])
    kv_full = pl.BlockSpec((None, seq, D_KV_WIN), lambda b, t: (b, 0, 0))
    q_tile = pl.BlockSpec((None, Q_TILE, D_MIX), lambda b, t: (b, t, 0))
    o = pl.pallas_call(
        functools.partial(_win_attn_kernel, length=seq),
        grid=(batch, seq // Q_TILE),
        in_specs=[pl.BlockSpec(memory_space=pltpu.SMEM), q_tile, kv_full, kv_full,
                  pl.BlockSpec(bias.shape, lambda b, t: (0, 0, 0, 0))],
        out_specs=q_tile,
        out_shape=jax.ShapeDtypeStruct((batch, seq, D_MIX), _BF16),
        compiler_params=pltpu.CompilerParams(dimension_semantics=("arbitrary",) * 2,
                                             vmem_limit_bytes=VMEM_LIMIT),
        name="win_attn",
    )(sink, view(q), view(k), view(v), bias)
    return o.reshape(batch * seq, D_MIX)


def _mix_ffn_kernel(x_ref, o1_ref, o2_ref, o3_ref, s1_ref, s2_ref, s3_ref, yb_ref, e_ref,
                    gd_ref, gw_ref, wout_ref, gf_ref, wg_ref, wu_ref, wd_ref, gfin_ref,
                    out_ref, act_ref, *, final):
    tm = x_ref.shape[0]
    lane = lax.broadcasted_iota(jnp.int32, (tm, LANES), 1)
    stats = [s1_ref[...], s2_ref[...], s3_ref[...]]
    big = jnp.maximum(jnp.maximum(stats[0], stats[1]), stats[2])
    weights = [jnp.exp(st - big) * pltpu.roll(st, LANES - N_HEADS, axis=1) for st in stats]
    inv = 1.0 / (weights[0] + weights[1] + weights[2])
    ya = jnp.zeros((tm, D_MIX), _F32)
    for w, o_ref in zip(weights, (o1_ref, o2_ref, o3_ref)):
        c = jnp.where(lane < N_HEADS, w * inv, 0.0)
        c_hi = c.astype(_BF16)
        c_lo = (c - c_hi.astype(_F32)).astype(_BF16)
        spread = (jnp.dot(c_hi, e_ref[...], preferred_element_type=_F32)
                  + jnp.dot(c_lo, e_ref[...], preferred_element_type=_F32))
        ya = ya + spread * o_ref[...].astype(_F32)
    ya = _rmsnorm(ya, gd_ref[...]).astype(_BF16)
    yb = _rmsnorm(yb_ref[...].astype(_F32), gw_ref[...]).astype(_BF16)
    x1 = (x_ref[...]
          + jnp.dot(ya, wout_ref[:D_MIX, :], preferred_element_type=_F32)
          + jnp.dot(yb, wout_ref[D_MIX:, :], preferred_element_type=_F32))
    h = _rmsnorm(x1, gf_ref[...]).astype(_BF16)
    for c0 in range(0, D_FF, FF_CHUNK):
        g = jnp.dot(h, wg_ref[:, c0:c0 + FF_CHUNK], preferred_element_type=_F32)
        u = jnp.dot(h, wu_ref[:, c0:c0 + FF_CHUNK], preferred_element_type=_F32)
        act_ref[:, c0:c0 + FF_CHUNK] = (g * jax.nn.sigmoid(g) * u).astype(_BF16)
    out = x1 + jnp.dot(act_ref[...], wd_ref[...], preferred_element_type=_F32)
    if final:
        out = _rmsnorm(out, gfin_ref[...])
    out_ref[...] = out


def _mix_ffn(x2, outs, stats, yb, spread, gd, gw, wout, gf, wg, wu, wd, gfin, final):
    n = x2.shape[0]
    tm = TOK_TILE_FFN
    row = lambda width: pl.BlockSpec((tm, width), lambda i: (i, 0))
    whole = lambda a: pl.BlockSpec(a.shape, lambda i: (0, 0), pipeline_mode=pl.Buffered(1))
    return pl.pallas_call(
        functools.partial(_mix_ffn_kernel, final=final),
        grid=(n // tm,),
        in_specs=[row(D_MODEL), row(D_MIX), row(D_MIX), row(D_MIX),
                  row(LANES), row(LANES), row(LANES), row(D_MIX),
                  whole(spread), whole(gd), whole(gw), whole(wout), whole(gf),
                  whole(wg), whole(wu), whole(wd), whole(gfin)],
        out_specs=row(D_MODEL),
        out_shape=jax.ShapeDtypeStruct((n, D_MODEL), _F32),
        scratch_shapes=[pltpu.VMEM((tm, D_FF), _BF16)],
        compiler_params=pltpu.CompilerParams(dimension_semantics=("arbitrary",),
                                             vmem_limit_bytes=VMEM_LIMIT),
        name="mix_ffn",
    )(x2, *outs, *stats, yb, spread, gd, gw, wout, gf, wg, wu, wd, gfin)


def kernel(x, g_mix, w_in, g_out_dil, g_out_win, sink, w_out, g_ffn, w_gate, w_up, w_down, g_final):
    batch, seq, _ = x.shape
    depth = w_in.shape[0]
    slopes = _alibi_slopes(N_ALIBI)
    order = np.array(WIN_HEAD_ORDER)
    slopes_win = slopes[:N_HEADS][order]
    slopes_dil = slopes[N_HEADS:]
    head_cols = (order[:, None] * HEAD_DIM + np.arange(HEAD_DIM)[None, :]).reshape(-1)

    bias_dil = [jnp.asarray(_bias_table(w // (2 * d), d, slopes_dil)) for w, d in DIL_PATTERNS]
    bias_win = jnp.asarray(_bias_table(WIN_SIDE, 1, slopes_win))
    spread = np.zeros((LANES, D_MIX), np.float32)
    for h in range(N_HEADS):
        spread[h, h * HEAD_DIM:(h + 1) * HEAD_DIM] = 1.0
    spread = jnp.asarray(spread, _BF16)

    qb0 = 3 * D_MIX
    in_cols = np.concatenate([np.arange(qb0), qb0 + head_cols, np.arange(qb0 + D_MIX, w_in.shape[-1])])
    w_in_k = w_in[:, :, in_cols].astype(_BF16)
    out_rows = np.concatenate([np.arange(D_MIX), D_MIX + head_cols])
    w_out_k = w_out[:, out_rows, :].astype(_BF16)
    g_win_k = g_out_win[:, head_cols]
    sink_k = sink[:, order]
    w_gate_k, w_up_k, w_down_k = (w.astype(_BF16) for w in (w_gate, w_up, w_down))
    vec = lambda g: g.reshape(1, -1)

    x2 = x.reshape(batch * seq, D_MODEL)
    for i in range(depth):
        qa, ka, va, qb, kb, vb = _inproj(x2, vec(g_mix[i]), w_in_k[i])
        outs, stats = [], []
        for (window, dil), bias in zip(DIL_PATTERNS, bias_dil):
            o, st = _dil_attention(qa, ka, va, bias, batch, seq, window, dil)
            outs.append(o)
            stats.append(st)
        yb = _win_attention(qb, kb, vb, bias_win, sink_k[i], batch, seq)
        x2 = _mix_ffn(x2, outs, stats, yb, spread, vec(g_out_dil[i]), vec(g_win_k[i]), w_out_k[i],
                      vec(g_ffn[i]), w_gate_k[i], w_up_k[i], w_down_k[i], vec(g_final),
                      final=(i == depth - 1))
    return x2.reshape(batch, seq, D_MODEL)
```

```python
import functools

import jax
import jax.numpy as jnp
import numpy as np
from jax import lax
from jax.experimental import pallas as pl
from jax.experimental.pallas import tpu as pltpu

D_MODEL = 1024
HEAD_DIM = 64
N_HEADS = 8
N_KV_WIN = 2
DIL_PATTERNS = ((128, 1), (512, 4), (2048, 16))
WIN_SIDE = 128
D_MIX = N_HEADS * HEAD_DIM
D_KV_WIN = N_KV_WIN * HEAD_DIM
D_FF = 2816
EPS = 1e-6
NEG = -1e30
N_ALIBI = 2 * N_HEADS

LANES = 128
FF_CHUNK = 256
Q_BLOCK = 128
Q_TILE = 1024
TOK_TILE_PROJ = 512
TOK_TILE_FFN = 512
VMEM_LIMIT = 56 * 1024 * 1024

WIN_HEAD_ORDER = (0, 4, 1, 5, 2, 6, 3, 7)

_F32 = jnp.float32
_BF16 = jnp.bfloat16


def _alibi_slopes(n):
    return np.array([2.0 ** (-8.0 * (i + 1) / n) for i in range(n)], dtype=np.float32)


def _bias_table(side, dil, slopes):
    kw = Q_BLOCK + 2 * side
    i = np.arange(Q_BLOCK)[:, None]
    j = np.arange(kw)[None, :]
    out = np.empty((3, len(slopes), Q_BLOCK, kw), np.float32)
    for v in range(3):
        rel = np.abs(j - i - side * v)
        dist = rel.astype(np.float32) * np.float32(dil)
        for h, sl in enumerate(slopes):
            out[v, h] = np.where(rel <= side, -(np.float32(sl) * dist), np.float32(NEG))
    return out


def _rmsnorm(x, g):
    return x * lax.rsqrt(jnp.mean(x * x, axis=-1, keepdims=True) + EPS) * g


def _inproj_kernel(x_ref, g_ref, w_ref, q1_ref, k1_ref, v1_ref, q4_ref, k4_ref, v4_ref,
                   q16_ref, k16_ref, v16_ref, qb_ref, kb_ref, vb_ref, nat_ref, cls_ref):
    tm = x_ref.shape[0]
    h = _rmsnorm(x_ref[...], g_ref[...]).astype(_BF16)
    scale = HEAD_DIM ** -0.5

    def project(col, width, mul):
        p = jnp.dot(h, w_ref[:, col:col + width], preferred_element_type=_F32)
        return p if mul is None else p * mul

    dilated = ((q1_ref, q4_ref, q16_ref, scale), (k1_ref, k4_ref, k16_ref, None),
               (v1_ref, v4_ref, v16_ref, None))
    for n, (o1_ref, o4_ref, o16_ref, mul) in enumerate(dilated):
        p = project(n * D_MIX, D_MIX, mul)
        o1_ref[...] = p.astype(_BF16)
        for j in range(D_MIX // LANES):
            cols = slice(j * LANES, (j + 1) * LANES)
            nat_ref[n, j] = p[:, cols]
            for c in range(4):
                rows = nat_ref[n, j, pl.ds(c, tm // 4, stride=4), :]
                o4_ref[c, :, cols] = rows.astype(_BF16)
                cls_ref[n, j, c] = rows
            for c in range(4):
                for b in range(4):
                    rows = cls_ref[n, j, c, pl.ds(b, tm // 16, stride=4), :]
                    o16_ref[4 * b + c, :, cols] = rows.astype(_BF16)
    col = 3 * D_MIX
    for ref, width, mul in ((qb_ref, D_MIX, scale), (kb_ref, D_KV_WIN, None), (vb_ref, D_KV_WIN, None)):
        ref[...] = project(col, width, mul).astype(_BF16)
        col += width


def _class_block(tm, dil, width, tiles_per_batch):
    return pl.BlockSpec((None, dil, tm // dil, width),
                        lambda i: (i // tiles_per_batch, 0, i % tiles_per_batch, 0))


def _inproj(x2, g, w, batch, seq):
    n = x2.shape[0]
    tm = TOK_TILE_PROJ
    tpb = seq // tm
    row = lambda width: pl.BlockSpec((tm, width), lambda i: (i, 0))
    flat = lambda width: jax.ShapeDtypeStruct((n, width), _BF16)
    cls = lambda dil: jax.ShapeDtypeStruct((batch, dil, seq // dil, D_MIX), _BF16)
    return pl.pallas_call(
        _inproj_kernel,
        grid=(n // tm,),
        in_specs=[row(D_MODEL),
                  pl.BlockSpec((1, D_MODEL), lambda i: (0, 0)),
                  pl.BlockSpec(w.shape, lambda i: (0, 0))],
        out_specs=([row(D_MIX)] * 3 + [_class_block(tm, 4, D_MIX, tpb)] * 3
                   + [_class_block(tm, 16, D_MIX, tpb)] * 3
                   + [row(D_MIX), row(D_KV_WIN), row(D_KV_WIN)]),
        out_shape=([flat(D_MIX)] * 3 + [cls(4)] * 3 + [cls(16)] * 3
                   + [flat(D_MIX), flat(D_KV_WIN), flat(D_KV_WIN)]),
        scratch_shapes=[pltpu.VMEM((3, D_MIX // LANES, tm, LANES), _F32),
                        pltpu.VMEM((3, D_MIX // LANES, 4, tm // 4, LANES), _F32)],
        compiler_params=pltpu.CompilerParams(dimension_semantics=("arbitrary",),
                                             vmem_limit_bytes=VMEM_LIMIT),
        name="inproj",
    )(x2, g, w)


def _block_window(t, i, side, length):
    kw = Q_BLOCK + 2 * side
    q0 = t * Q_TILE + i * Q_BLOCK
    kstart = jnp.clip(q0 - side, 0, length - kw)
    var = lax.shift_right_logical(q0 - kstart, side.bit_length() - 1)
    return pl.multiple_of(i * Q_BLOCK, Q_BLOCK), pl.multiple_of(kstart, side), var


def _dil_attn_kernel(q_ref, k_ref, v_ref, bias_ref, o_ref, st_ref, *, side, length):
    t = pl.program_id(2)
    kw = Q_BLOCK + 2 * side
    lane = lax.broadcasted_iota(jnp.int32, (Q_BLOCK, LANES), 1)
    lo = lane < HEAD_DIM

    def body(i, carry):
        r0, kstart, var = _block_window(t, i, side, length)
        st = jnp.zeros((Q_BLOCK, LANES), _F32)
        for pair in range(N_HEADS // 2):
            cols = slice(pair * LANES, (pair + 1) * LANES)
            qp = q_ref[pl.ds(r0, Q_BLOCK), cols]
            kp = k_ref[pl.ds(kstart, kw), cols]
            vp = v_ref[pl.ds(kstart, kw), cols]
            halves = []
            for half in range(2):
                h = 2 * pair + half
                qm = jnp.where(lo if half == 0 else jnp.logical_not(lo), qp, jnp.zeros_like(qp))
                s = lax.dot_general(qm, kp, (((1,), (1,)), ((), ())), preferred_element_type=_F32)
                s = s + bias_ref[var, h]
                m = jnp.max(s, axis=-1, keepdims=True)
                p = jnp.exp(s - m)
                l = jnp.sum(p, axis=-1, keepdims=True)
                o = jnp.dot(p.astype(_BF16), vp, preferred_element_type=_F32)
                halves.append(o * (1.0 / l))
                st = jnp.where(lane == h, m, st)
                st = jnp.where(lane == N_HEADS + h, l, st)
            o_ref[pl.ds(r0, Q_BLOCK), cols] = jnp.where(lo, halves[0], halves[1]).astype(_BF16)
        st_ref[pl.ds(r0, Q_BLOCK), :] = st
        return carry

    lax.fori_loop(0, min(Q_TILE, length) // Q_BLOCK, body, 0)


def _dil_attention(q, k, v, bias, window, dil):
    batch, _, length, _ = q.shape
    side = window // (2 * dil)
    tq = min(Q_TILE, length)
    tile = lambda width: pl.BlockSpec((None, None, tq, width), lambda b, r, t: (b, r, t, 0))
    kv_full = pl.BlockSpec((None, None, length, D_MIX), lambda b, r, t: (b, r, 0, 0))
    return pl.pallas_call(
        functools.partial(_dil_attn_kernel, side=side, length=length),
        grid=(batch, dil, length // tq),
        in_specs=[tile(D_MIX), kv_full, kv_full,
                  pl.BlockSpec(bias.shape, lambda b, r, t: (0, 0, 0, 0))],
        out_specs=[tile(D_MIX), tile(LANES)],
        out_shape=[jax.ShapeDtypeStruct((batch, dil, length, D_MIX), _BF16),
                   jax.ShapeDtypeStruct((batch, dil, length, LANES), _F32)],
        compiler_params=pltpu.CompilerParams(dimension_semantics=("arbitrary",) * 3,
                                             vmem_limit_bytes=VMEM_LIMIT),
        name=f"dil_attn_d{dil}",
    )(q, k, v, bias)


def _win_attn_kernel(sink_ref, q_ref, k_ref, v_ref, bias_ref, o_ref, *, length):
    t = pl.program_id(1)
    side = WIN_SIDE
    kw = Q_BLOCK + 2 * side
    lane = lax.broadcasted_iota(jnp.int32, (Q_BLOCK, LANES), 1)
    lo = lane < HEAD_DIM

    def body(i, carry):
        r0, kstart, var = _block_window(t, i, side, length)
        kp = k_ref[pl.ds(kstart, kw), :]
        vp = v_ref[pl.ds(kstart, kw), :]
        for pair in range(N_HEADS // 2):
            cols = slice(pair * LANES, (pair + 1) * LANES)
            qp = q_ref[pl.ds(r0, Q_BLOCK), cols]
            halves = []
            for half in range(2):
                h = 2 * pair + half
                qm = jnp.where(lo if half == 0 else jnp.logical_not(lo), qp, jnp.zeros_like(qp))
                s = lax.dot_general(qm, kp, (((1,), (1,)), ((), ())), preferred_element_type=_F32)
                s = s + bias_ref[var, h]
                m = jnp.max(s, axis=-1, keepdims=True)
                p = jnp.exp(s - m)
                l = jnp.sum(p, axis=-1, keepdims=True)
                o = jnp.dot(p.astype(_BF16), vp, preferred_element_type=_F32)
                sk = sink_ref[h]
                big = jnp.maximum(m, sk)
                a = jnp.exp(m - big)
                den = l * a + jnp.exp(sk - big)
                halves.append(o * (a / den))
            o_ref[pl.ds(r0, Q_BLOCK), cols] = jnp.where(lo, halves[0], halves[1]).astype(_BF16)
        return carry

    lax.fori_loop(0, Q_TILE // Q_BLOCK, body, 0)


def _win_attention(q, k, v, bias, sink, batch, seq):
    view = lambda a: a.reshape(batch, seq, a.shape[-1])
    kv_full = pl.BlockSpec((None, seq, D_KV_WIN), lambda b, t: (b, 0, 0))
    q_tile = pl.BlockSpec((None, Q_TILE, D_MIX), lambda b, t: (b, t, 0))
    o = pl.pallas_call(
        functools.partial(_win_attn_kernel, length=seq),
        grid=(batch, seq // Q_TILE),
        in_specs=[pl.BlockSpec(memory_space=pltpu.SMEM), q_tile, kv_full, kv_full,
                  pl.BlockSpec(bias.shape, lambda b, t: (0, 0, 0, 0))],
        out_specs=q_tile,
        out_shape=jax.ShapeDtypeStruct((batch, seq, D_MIX), _BF16),
        compiler_params=pltpu.CompilerParams(dimension_semantics=("arbitrary",) * 2,
                                             vmem_limit_bytes=VMEM_LIMIT),
        name="win_attn",
    )(sink, view(q), view(k), view(v), bias)
    return o.reshape(batch * seq, D_MIX)


def _to_token_order(src4_ref, src16_ref, nat4_ref, nat16_ref, cls_ref):
    n_tiles, tm, _ = nat4_ref.shape
    for j in range(n_tiles):
        cols = slice(j * LANES, (j + 1) * LANES)
        for c in range(4):
            nat4_ref[j, pl.ds(c, tm // 4, stride=4), :] = src4_ref[c, :, cols].astype(_F32)
            for b in range(4):
                cls_ref[j, c, pl.ds(b, tm // 16, stride=4), :] = (
                    src16_ref[4 * b + c, :, cols].astype(_F32))
        for c in range(4):
            nat16_ref[j, pl.ds(c, tm // 4, stride=4), :] = cls_ref[j, c]


def _lane_tiles(ref):
    return jnp.concatenate([ref[j] for j in range(ref.shape[0])], axis=-1)


def _mix_ffn_kernel(x_ref, o1_ref, o4_ref, o16_ref, s1_ref, s4_ref, s16_ref, yb_ref, e_ref,
                    gd_ref, gw_ref, wout_ref, gf_ref, wg_ref, wu_ref, wd_ref, gfin_ref,
                    out_ref, act_ref, on4_ref, on16_ref, ocls_ref, sn4_ref, sn16_ref, scls_ref,
                    *, final):
    tm = x_ref.shape[0]
    _to_token_order(o4_ref, o16_ref, on4_ref, on16_ref, ocls_ref)
    _to_token_order(s4_ref, s16_ref, sn4_ref, sn16_ref, scls_ref)
    lane = lax.broadcasted_iota(jnp.int32, (tm, LANES), 1)
    stats = [s1_ref[...], sn4_ref[0], sn16_ref[0]]
    outs = [o1_ref[...].astype(_F32), _lane_tiles(on4_ref), _lane_tiles(on16_ref)]
    big = jnp.maximum(jnp.maximum(stats[0], stats[1]), stats[2])
    weights = [jnp.exp(st - big) * pltpu.roll(st, LANES - N_HEADS, axis=1) for st in stats]
    inv = 1.0 / (weights[0] + weights[1] + weights[2])
    ya = jnp.zeros((tm, D_MIX), _F32)
    for w, o in zip(weights, outs):
        c = jnp.where(lane < N_HEADS, w * inv, 0.0)
        c_hi = c.astype(_BF16)
        c_lo = (c - c_hi.astype(_F32)).astype(_BF16)
        spread = (jnp.dot(c_hi, e_ref[...], preferred_element_type=_F32)
                  + jnp.dot(c_lo, e_ref[...], preferred_element_type=_F32))
        ya = ya + spread * o
    ya = _rmsnorm(ya, gd_ref[...]).astype(_BF16)
    yb = _rmsnorm(yb_ref[...].astype(_F32), gw_ref[...]).astype(_BF16)
    x1 = (x_ref[...]
          + jnp.dot(ya, wout_ref[:D_MIX, :], preferred_element_type=_F32)
          + jnp.dot(yb, wout_ref[D_MIX:, :], preferred_element_type=_F32))
    h = _rmsnorm(x1, gf_ref[...]).astype(_BF16)
    for c0 in range(0, D_FF, FF_CHUNK):
        g = jnp.dot(h, wg_ref[:, c0:c0 + FF_CHUNK], preferred_element_type=_F32)
        u = jnp.dot(h, wu_ref[:, c0:c0 + FF_CHUNK], preferred_element_type=_F32)
        act_ref[:, c0:c0 + FF_CHUNK] = (g * jax.nn.sigmoid(g) * u).astype(_BF16)
    out = x1 + jnp.dot(act_ref[...], wd_ref[...], preferred_element_type=_F32)
    if final:
        out = _rmsnorm(out, gfin_ref[...])
    out_ref[...] = out


def _mix_ffn(x2, outs, stats, yb, spread, gd, gw, wout, gf, wg, wu, wd, gfin, seq, final):
    n = x2.shape[0]
    tm = TOK_TILE_FFN
    tpb = seq // tm
    row = lambda width: pl.BlockSpec((tm, width), lambda i: (i, 0))
    whole = lambda a: pl.BlockSpec(a.shape, lambda i: (0, 0), pipeline_mode=pl.Buffered(1))
    per_pattern = lambda width: [row(width), _class_block(tm, 4, width, tpb),
                                 _class_block(tm, 16, width, tpb)]
    nat = lambda width: pltpu.VMEM((width // LANES, tm, LANES), _F32)
    cls = lambda width: pltpu.VMEM((width // LANES, 4, tm // 4, LANES), _F32)
    return pl.pallas_call(
        functools.partial(_mix_ffn_kernel, final=final),
        grid=(n // tm,),
        in_specs=[row(D_MODEL), *per_pattern(D_MIX), *per_pattern(LANES), row(D_MIX),
                  whole(spread), whole(gd), whole(gw), whole(wout), whole(gf),
                  whole(wg), whole(wu), whole(wd), whole(gfin)],
        out_specs=row(D_MODEL),
        out_shape=jax.ShapeDtypeStruct((n, D_MODEL), _F32),
        scratch_shapes=[pltpu.VMEM((tm, D_FF), _BF16), nat(D_MIX), nat(D_MIX), cls(D_MIX),
                        nat(LANES), nat(LANES), cls(LANES)],
        compiler_params=pltpu.CompilerParams(dimension_semantics=("arbitrary",),
                                             vmem_limit_bytes=VMEM_LIMIT),
        name="mix_ffn",
    )(x2, *outs, *stats, yb, spread, gd, gw, wout, gf, wg, wu, wd, gfin)


def kernel(x, g_mix, w_in, g_out_dil, g_out_win, sink, w_out, g_ffn, w_gate, w_up, w_down, g_final):
    batch, seq, _ = x.shape
    depth = w_in.shape[0]
    slopes = _alibi_slopes(N_ALIBI)
    order = np.array(WIN_HEAD_ORDER)
    slopes_win = slopes[:N_HEADS][order]
    slopes_dil = slopes[N_HEADS:]
    head_cols = (order[:, None] * HEAD_DIM + np.arange(HEAD_DIM)[None, :]).reshape(-1)

    bias_dil = [jnp.asarray(_bias_table(w // (2 * d), d, slopes_dil)) for w, d in DIL_PATTERNS]
    bias_win = jnp.asarray(_bias_table(WIN_SIDE, 1, slopes_win))
    spread = np.zeros((LANES, D_MIX), np.float32)
    for h in range(N_HEADS):
        spread[h, h * HEAD_DIM:(h + 1) * HEAD_DIM] = 1.0
    spread = jnp.asarray(spread, _BF16)

    qb0 = 3 * D_MIX
    in_cols = np.concatenate([np.arange(qb0), qb0 + head_cols, np.arange(qb0 + D_MIX, w_in.shape[-1])])
    w_in_k = w_in[:, :, in_cols].astype(_BF16)
    out_rows = np.concatenate([np.arange(D_MIX), D_MIX + head_cols])
    w_out_k = w_out[:, out_rows, :].astype(_BF16)
    g_win_k = g_out_win[:, head_cols]
    sink_k = sink[:, order]
    w_gate_k, w_up_k, w_down_k = (w.astype(_BF16) for w in (w_gate, w_up, w_down))
    vec = lambda g: g.reshape(1, -1)

    x2 = x.reshape(batch * seq, D_MODEL)
    for i in range(depth):
        q1, k1, v1, q4, k4, v4, q16, k16, v16, qb, kb, vb = _inproj(
            x2, vec(g_mix[i]), w_in_k[i], batch, seq)
        as_class = lambda a: a.reshape(batch, 1, seq, D_MIX)
        qkv = ((as_class(q1), as_class(k1), as_class(v1)), (q4, k4, v4), (q16, k16, v16))
        outs, stats = [], []
        for (window, dil), (q, k, v), bias in zip(DIL_PATTERNS, qkv, bias_dil):
            o, st = _dil_attention(q, k, v, bias, window, dil)
            outs.append(o)
            stats.append(st)
        outs[0] = outs[0].reshape(batch * seq, D_MIX)
        stats[0] = stats[0].reshape(batch * seq, LANES)
        yb = _win_attention(qb, kb, vb, bias_win, sink_k[i], batch, seq)
        x2 = _mix_ffn(x2, outs, stats, yb, spread, vec(g_out_dil[i]), vec(g_win_k[i]), w_out_k[i],
                      vec(g_ffn[i]), w_gate_k[i], w_up_k[i], w_down_k[i], vec(g_final),
                      seq, final=(i == depth - 1))
    return x2.reshape(batch, seq, D_MODEL)
```

```python
import functools

import jax
import jax.numpy as jnp
import numpy as np
from jax import lax
from jax.experimental import pallas as pl
from jax.experimental.pallas import tpu as pltpu

D_MODEL = 1024
HEAD_DIM = 64
N_HEADS = 8
N_KV_WIN = 2
DIL_PATTERNS = ((128, 1), (512, 4), (2048, 16))
WIN_SIDE = 128
D_MIX = N_HEADS * HEAD_DIM
D_KV_WIN = N_KV_WIN * HEAD_DIM
D_FF = 2816
EPS = 1e-6
NEG = -1e30
N_ALIBI = 2 * N_HEADS

LANES = 128
FF_CHUNK = 256
Q_BLOCK = 128
Q_TILE = 1024
TOK_TILE_PROJ = 512
TOK_TILE_FFN = 512
VMEM_LIMIT = 56 * 1024 * 1024

WIN_HEAD_ORDER = (0, 4, 1, 5, 2, 6, 3, 7)

_F32 = jnp.float32
_BF16 = jnp.bfloat16


def _alibi_slopes(n):
    return np.array([2.0 ** (-8.0 * (i + 1) / n) for i in range(n)], dtype=np.float32)


def _bias_table(side, dil, slopes):
    kw = Q_BLOCK + 2 * side
    i = np.arange(Q_BLOCK)[:, None]
    j = np.arange(kw)[None, :]
    out = np.empty((3, len(slopes), Q_BLOCK, kw), np.float32)
    for v in range(3):
        rel = np.abs(j - i - side * v)
        dist = rel.astype(np.float32) * np.float32(dil)
        for h, sl in enumerate(slopes):
            out[v, h] = np.where(rel <= side, -(np.float32(sl) * dist), np.float32(NEG))
    return out.reshape(3, len(slopes) * Q_BLOCK, kw)


def _rmsnorm(x, g):
    return x * lax.rsqrt(jnp.mean(x * x, axis=-1, keepdims=True) + EPS) * g


def _inproj_kernel(x_ref, g_ref, w_ref, q1_ref, k1_ref, v1_ref, q4_ref, k4_ref, v4_ref,
                   q16_ref, k16_ref, v16_ref, qb_ref, kb_ref, vb_ref, nat_ref, cls_ref):
    tm = x_ref.shape[0]
    h = _rmsnorm(x_ref[...], g_ref[...]).astype(_BF16)
    scale = HEAD_DIM ** -0.5

    def project(col, width, mul):
        p = jnp.dot(h, w_ref[:, col:col + width], preferred_element_type=_F32)
        return p if mul is None else p * mul

    dilated = ((q1_ref, q4_ref, q16_ref, scale), (k1_ref, k4_ref, k16_ref, None),
               (v1_ref, v4_ref, v16_ref, None))
    for n, (o1_ref, o4_ref, o16_ref, mul) in enumerate(dilated):
        p = project(n * D_MIX, D_MIX, mul)
        o1_ref[...] = p.astype(_BF16)
        for j in range(D_MIX // LANES):
            cols = slice(j * LANES, (j + 1) * LANES)
            nat_ref[n, j] = p[:, cols]
            for c in range(4):
                rows = nat_ref[n, j, pl.ds(c, tm // 4, stride=4), :]
                o4_ref[c, :, cols] = rows.astype(_BF16)
                cls_ref[n, j, c] = rows
            for c in range(4):
                for b in range(4):
                    rows = cls_ref[n, j, c, pl.ds(b, tm // 16, stride=4), :]
                    o16_ref[4 * b + c, :, cols] = rows.astype(_BF16)
    col = 3 * D_MIX
    for ref, width, mul in ((qb_ref, D_MIX, scale), (kb_ref, D_KV_WIN, None), (vb_ref, D_KV_WIN, None)):
        ref[...] = project(col, width, mul).astype(_BF16)
        col += width


def _class_block(tm, dil, width, tiles_per_batch):
    return pl.BlockSpec((None, dil, tm // dil, width),
                        lambda i: (i // tiles_per_batch, 0, i % tiles_per_batch, 0))


def _inproj(x2, g, w, batch, seq):
    n = x2.shape[0]
    tm = TOK_TILE_PROJ
    tpb = seq // tm
    row = lambda width: pl.BlockSpec((tm, width), lambda i: (i, 0))
    flat = lambda width: jax.ShapeDtypeStruct((n, width), _BF16)
    cls = lambda dil: jax.ShapeDtypeStruct((batch, dil, seq // dil, D_MIX), _BF16)
    return pl.pallas_call(
        _inproj_kernel,
        grid=(n // tm,),
        in_specs=[row(D_MODEL),
                  pl.BlockSpec((1, D_MODEL), lambda i: (0, 0)),
                  pl.BlockSpec(w.shape, lambda i: (0, 0))],
        out_specs=([row(D_MIX)] * 3 + [_class_block(tm, 4, D_MIX, tpb)] * 3
                   + [_class_block(tm, 16, D_MIX, tpb)] * 3
                   + [row(D_MIX), row(D_KV_WIN), row(D_KV_WIN)]),
        out_shape=([flat(D_MIX)] * 3 + [cls(4)] * 3 + [cls(16)] * 3
                   + [flat(D_MIX), flat(D_KV_WIN), flat(D_KV_WIN)]),
        scratch_shapes=[pltpu.VMEM((3, D_MIX // LANES, tm, LANES), _F32),
                        pltpu.VMEM((3, D_MIX // LANES, 4, tm // 4, LANES), _F32)],
        compiler_params=pltpu.CompilerParams(dimension_semantics=("arbitrary",),
                                             vmem_limit_bytes=VMEM_LIMIT),
        name="inproj",
    )(x2, g, w)


def _block_window(t, i, side, length):
    kw = Q_BLOCK + 2 * side
    q0 = t * Q_TILE + i * Q_BLOCK
    kstart = jnp.clip(q0 - side, 0, length - kw)
    var = lax.shift_right_logical(q0 - kstart, side.bit_length() - 1)
    return pl.multiple_of(i * Q_BLOCK, Q_BLOCK), pl.multiple_of(kstart, side), var


def _split_heads(qp, lo):
    zero = jnp.zeros_like(qp)
    return jnp.concatenate([jnp.where(lo, qp, zero), jnp.where(lo, zero, qp)], axis=0)


def _with_ones(vp):
    return jnp.concatenate([vp, jnp.ones_like(vp)], axis=1)


def _dil_attn_kernel(q_ref, k_ref, v_ref, bias_ref, o_ref, st_ref, *, side, length):
    t = pl.program_id(2)
    kw = Q_BLOCK + 2 * side
    n_pairs = N_HEADS // 2
    lane = lax.broadcasted_iota(jnp.int32, (Q_BLOCK, LANES), 1)
    lo = lane < HEAD_DIM

    def body(i, carry):
        r0, kstart, var = _block_window(t, i, side, length)
        pair_cols = [slice(p * LANES, (p + 1) * LANES) for p in range(n_pairs)]
        s = jnp.concatenate(
            [lax.dot_general(_split_heads(q_ref[pl.ds(r0, Q_BLOCK), c], lo), k_ref[pl.ds(kstart, kw), c],
                             (((1,), (1,)), ((), ())), preferred_element_type=_F32)
             for c in pair_cols], axis=0)
        s = s + bias_ref[var]
        m = jnp.max(s, axis=-1, keepdims=True)
        p = jnp.exp(s - m).astype(_BF16)
        st = jnp.zeros((Q_BLOCK, LANES), _F32)
        for pair, c in enumerate(pair_cols):
            rows = slice(2 * pair * Q_BLOCK, 2 * (pair + 1) * Q_BLOCK)
            ol = jnp.dot(p[rows], _with_ones(v_ref[pl.ds(kstart, kw), c]), preferred_element_type=_F32)
            l = ol[:, LANES:]
            o = ol[:, :LANES] * (1.0 / l)
            o_ref[pl.ds(r0, Q_BLOCK), c] = jnp.where(lo, o[:Q_BLOCK], o[Q_BLOCK:]).astype(_BF16)
            for half in range(2):
                h = 2 * pair + half
                hrows = slice(half * Q_BLOCK, (half + 1) * Q_BLOCK)
                st = jnp.where(lane == h, m[h * Q_BLOCK:(h + 1) * Q_BLOCK], st)
                st = jnp.where(lane == N_HEADS + h, l[hrows], st)
        st_ref[pl.ds(r0, Q_BLOCK), :] = st
        return carry

    lax.fori_loop(0, min(Q_TILE, length) // Q_BLOCK, body, 0)


def _dil_attention(q, k, v, bias, window, dil):
    batch, _, length, _ = q.shape
    side = window // (2 * dil)
    tq = min(Q_TILE, length)
    tile = lambda width: pl.BlockSpec((None, None, tq, width), lambda b, r, t: (b, r, t, 0))
    kv_full = pl.BlockSpec((None, None, length, D_MIX), lambda b, r, t: (b, r, 0, 0))
    return pl.pallas_call(
        functools.partial(_dil_attn_kernel, side=side, length=length),
        grid=(batch, dil, length // tq),
        in_specs=[tile(D_MIX), kv_full, kv_full,
                  pl.BlockSpec(bias.shape, lambda b, r, t: (0, 0, 0))],
        out_specs=[tile(D_MIX), tile(LANES)],
        out_shape=[jax.ShapeDtypeStruct((batch, dil, length, D_MIX), _BF16),
                   jax.ShapeDtypeStruct((batch, dil, length, LANES), _F32)],
        compiler_params=pltpu.CompilerParams(dimension_semantics=("arbitrary",) * 3,
                                             vmem_limit_bytes=VMEM_LIMIT),
        name=f"dil_attn_d{dil}",
    )(q, k, v, bias)


def _win_attn_kernel(sink_ref, q_ref, k_ref, v_ref, bias_ref, o_ref, *, length):
    t = pl.program_id(1)
    side = WIN_SIDE
    kw = Q_BLOCK + 2 * side
    n_pairs = N_HEADS // 2
    lane = lax.broadcasted_iota(jnp.int32, (Q_BLOCK, LANES), 1)
    lo = lane < HEAD_DIM

    def body(i, carry):
        r0, kstart, var = _block_window(t, i, side, length)
        q = jnp.concatenate(
            [_split_heads(q_ref[pl.ds(r0, Q_BLOCK), p * LANES:(p + 1) * LANES], lo)
             for p in range(n_pairs)], axis=0)
        s = lax.dot_general(q, k_ref[pl.ds(kstart, kw), :], (((1,), (1,)), ((), ())),
                            preferred_element_type=_F32)
        s = s + bias_ref[var]
        sk = sink_ref[...]
        m = jnp.maximum(jnp.max(s, axis=-1, keepdims=True), sk)
        p = jnp.exp(s - m).astype(_BF16)
        ol = jnp.dot(p, _with_ones(v_ref[pl.ds(kstart, kw), :]), preferred_element_type=_F32)
        o = ol[:, :LANES] * (1.0 / (ol[:, LANES:] + jnp.exp(sk - m)))
        for pair in range(n_pairs):
            top = 2 * pair * Q_BLOCK
            o_ref[pl.ds(r0, Q_BLOCK), pair * LANES:(pair + 1) * LANES] = jnp.where(
                lo, o[top:top + Q_BLOCK], o[top + Q_BLOCK:top + 2 * Q_BLOCK]).astype(_BF16)
        return carry

    lax.fori_loop(0, Q_TILE // Q_BLOCK, body, 0)


def _win_attention(q, k, v, bias, sink, batch, seq):
    view = lambda a: a.reshape(batch, seq, a.shape[-1])
    kv_full = pl.BlockSpec((None, seq, D_KV_WIN), lambda b, t: (b, 0, 0))
    q_tile = pl.BlockSpec((None, Q_TILE, D_MIX), lambda b, t: (b, t, 0))
    o = pl.pallas_call(
        functools.partial(_win_attn_kernel, length=seq),
        grid=(batch, seq // Q_TILE),
        in_specs=[pl.BlockSpec(sink.shape, lambda b, t: (0, 0)), q_tile, kv_full, kv_full,
                  pl.BlockSpec(bias.shape, lambda b, t: (0, 0, 0))],
        out_specs=q_tile,
        out_shape=jax.ShapeDtypeStruct((batch, seq, D_MIX), _BF16),
        compiler_params=pltpu.CompilerParams(dimension_semantics=("arbitrary",) * 2,
                                             vmem_limit_bytes=VMEM_LIMIT),
        name="win_attn",
    )(sink, view(q), view(k), view(v), bias)
    return o.reshape(batch * seq, D_MIX)


def _to_token_order(src4_ref, src16_ref, nat4_ref, nat16_ref, cls_ref):
    n_tiles, tm, _ = nat4_ref.shape
    for j in range(n_tiles):
        cols = slice(j * LANES, (j + 1) * LANES)
        for c in range(4):
            nat4_ref[j, pl.ds(c, tm // 4, stride=4), :] = src4_ref[c, :, cols].astype(_F32)
            for b in range(4):
                cls_ref[j, c, pl.ds(b, tm // 16, stride=4), :] = (
                    src16_ref[4 * b + c, :, cols].astype(_F32))
        for c in range(4):
            nat16_ref[j, pl.ds(c, tm // 4, stride=4), :] = cls_ref[j, c]


def _lane_tiles(ref):
    return jnp.concatenate([ref[j] for j in range(ref.shape[0])], axis=-1)


def _mix_ffn_kernel(x_ref, o1_ref, o4_ref, o16_ref, s1_ref, s4_ref, s16_ref, yb_ref, e_ref,
                    gd_ref, gw_ref, wout_ref, gf_ref, wg_ref, wu_ref, wd_ref, gfin_ref,
                    out_ref, act_ref, on4_ref, on16_ref, ocls_ref, sn4_ref, sn16_ref, scls_ref,
                    *, final):
    tm = x_ref.shape[0]
    _to_token_order(o4_ref, o16_ref, on4_ref, on16_ref, ocls_ref)
    _to_token_order(s4_ref, s16_ref, sn4_ref, sn16_ref, scls_ref)
    lane = lax.broadcasted_iota(jnp.int32, (tm, LANES), 1)
    stats = [s1_ref[...], sn4_ref[0], sn16_ref[0]]
    outs = [o1_ref[...].astype(_F32), _lane_tiles(on4_ref), _lane_tiles(on16_ref)]
    big = jnp.maximum(jnp.maximum(stats[0], stats[1]), stats[2])
    weights = [jnp.exp(st - big) * pltpu.roll(st, LANES - N_HEADS, axis=1) for st in stats]
    inv = 1.0 / (weights[0] + weights[1] + weights[2])
    ya = jnp.zeros((tm, D_MIX), _F32)
    for w, o in zip(weights, outs):
        c = jnp.where(lane < N_HEADS, w * inv, 0.0)
        c_hi = c.astype(_BF16)
        c_lo = (c - c_hi.astype(_F32)).astype(_BF16)
        spread = (jnp.dot(c_hi, e_ref[...], preferred_element_type=_F32)
                  + jnp.dot(c_lo, e_ref[...], preferred_element_type=_F32))
        ya = ya + spread * o
    ya = _rmsnorm(ya, gd_ref[...]).astype(_BF16)
    yb = _rmsnorm(yb_ref[...].astype(_F32), gw_ref[...]).astype(_BF16)
    x1 = (x_ref[...]
          + jnp.dot(ya, wout_ref[:D_MIX, :], preferred_element_type=_F32)
          + jnp.dot(yb, wout_ref[D_MIX:, :], preferred_element_type=_F32))
    h = _rmsnorm(x1, gf_ref[...]).astype(_BF16)
    for c0 in range(0, D_FF, FF_CHUNK):
        g = jnp.dot(h, wg_ref[:, c0:c0 + FF_CHUNK], preferred_element_type=_F32)
        u = jnp.dot(h, wu_ref[:, c0:c0 + FF_CHUNK], preferred_element_type=_F32)
        act_ref[:, c0:c0 + FF_CHUNK] = (g * jax.nn.sigmoid(g) * u).astype(_BF16)
    out = x1 + jnp.dot(act_ref[...], wd_ref[...], preferred_element_type=_F32)
    if final:
        out = _rmsnorm(out, gfin_ref[...])
    out_ref[...] = out


def _mix_ffn(x2, outs, stats, yb, spread, gd, gw, wout, gf, wg, wu, wd, gfin, seq, final):
    n = x2.shape[0]
    tm = TOK_TILE_FFN
    tpb = seq // tm
    row = lambda width: pl.BlockSpec((tm, width), lambda i: (i, 0))
    whole = lambda a: pl.BlockSpec(a.shape, lambda i: (0, 0), pipeline_mode=pl.Buffered(1))
    per_pattern = lambda width: [row(width), _class_block(tm, 4, width, tpb),
                                 _class_block(tm, 16, width, tpb)]
    nat = lambda width: pltpu.VMEM((width // LANES, tm, LANES), _F32)
    cls = lambda width: pltpu.VMEM((width // LANES, 4, tm // 4, LANES), _F32)
    return pl.pallas_call(
        functools.partial(_mix_ffn_kernel, final=final),
        grid=(n // tm,),
        in_specs=[row(D_MODEL), *per_pattern(D_MIX), *per_pattern(LANES), row(D_MIX),
                  whole(spread), whole(gd), whole(gw), whole(wout), whole(gf),
                  whole(wg), whole(wu), whole(wd), whole(gfin)],
        out_specs=row(D_MODEL),
        out_shape=jax.ShapeDtypeStruct((n, D_MODEL), _F32),
        scratch_shapes=[pltpu.VMEM((tm, D_FF), _BF16), nat(D_MIX), nat(D_MIX), cls(D_MIX),
                        nat(LANES), nat(LANES), cls(LANES)],
        compiler_params=pltpu.CompilerParams(dimension_semantics=("arbitrary",),
                                             vmem_limit_bytes=VMEM_LIMIT),
        name="mix_ffn",
    )(x2, *outs, *stats, yb, spread, gd, gw, wout, gf, wg, wu, wd, gfin)


def kernel(x, g_mix, w_in, g_out_dil, g_out_win, sink, w_out, g_ffn, w_gate, w_up, w_down, g_final):
    batch, seq, _ = x.shape
    depth = w_in.shape[0]
    slopes = _alibi_slopes(N_ALIBI)
    order = np.array(WIN_HEAD_ORDER)
    slopes_win = slopes[:N_HEADS][order]
    slopes_dil = slopes[N_HEADS:]
    head_cols = (order[:, None] * HEAD_DIM + np.arange(HEAD_DIM)[None, :]).reshape(-1)

    bias_dil = [jnp.asarray(_bias_table(w // (2 * d), d, slopes_dil)) for w, d in DIL_PATTERNS]
    bias_win = jnp.asarray(_bias_table(WIN_SIDE, 1, slopes_win))
    spread = np.zeros((LANES, D_MIX), np.float32)
    for h in range(N_HEADS):
        spread[h, h * HEAD_DIM:(h + 1) * HEAD_DIM] = 1.0
    spread = jnp.asarray(spread, _BF16)

    qb0 = 3 * D_MIX
    in_cols = np.concatenate([np.arange(qb0), qb0 + head_cols, np.arange(qb0 + D_MIX, w_in.shape[-1])])
    w_in_k = w_in[:, :, in_cols].astype(_BF16)
    out_rows = np.concatenate([np.arange(D_MIX), D_MIX + head_cols])
    w_out_k = w_out[:, out_rows, :].astype(_BF16)
    g_win_k = g_out_win[:, head_cols]
    sink_k = jnp.repeat(sink[:, order], Q_BLOCK, axis=1)[:, :, None]
    w_gate_k, w_up_k, w_down_k = (w.astype(_BF16) for w in (w_gate, w_up, w_down))
    vec = lambda g: g.reshape(1, -1)

    x2 = x.reshape(batch * seq, D_MODEL)
    for i in range(depth):
        q1, k1, v1, q4, k4, v4, q16, k16, v16, qb, kb, vb = _inproj(
            x2, vec(g_mix[i]), w_in_k[i], batch, seq)
        as_class = lambda a: a.reshape(batch, 1, seq, D_MIX)
        qkv = ((as_class(q1), as_class(k1), as_class(v1)), (q4, k4, v4), (q16, k16, v16))
        outs, stats = [], []
        for (window, dil), (q, k, v), bias in zip(DIL_PATTERNS, qkv, bias_dil):
            o, st = _dil_attention(q, k, v, bias, window, dil)
            outs.append(o)
            stats.append(st)
        outs[0] = outs[0].reshape(batch * seq, D_MIX)
        stats[0] = stats[0].reshape(batch * seq, LANES)
        yb = _win_attention(qb, kb, vb, bias_win, sink_k[i], batch, seq)
        x2 = _mix_ffn(x2, outs, stats, yb, spread, vec(g_out_dil[i]), vec(g_win_k[i]), w_out_k[i],
                      vec(g_ffn[i]), w_gate_k[i], w_up_k[i], w_down_k[i], vec(g_final),
                      seq, final=(i == depth - 1))
    return x2.reshape(batch, seq, D_MODEL)
```

```python
import functools
import math

import jax
import jax.numpy as jnp
import numpy as np
from jax import lax
from jax.experimental import pallas as pl
from jax.experimental.pallas import tpu as pltpu

D_MODEL = 1024
HEAD_DIM = 64
N_HEADS = 8
N_KV_WIN = 2
DIL_PATTERNS = ((128, 1), (512, 4), (2048, 16))
WIN_SIDE = 128
D_MIX = N_HEADS * HEAD_DIM
D_KV_WIN = N_KV_WIN * HEAD_DIM
D_FF = 2816
EPS = 1e-6
NEG = -1e30
N_ALIBI = 2 * N_HEADS
LOG2E = math.log2(math.e)

LANES = 128
FF_CHUNK = 256
Q_BLOCK = 128
Q_TILE = 1024
UNROLL_BLOCKS = 8
TOK_TILE_PROJ = 512
TOK_TILE_FFN = 512
VMEM_LIMIT = 56 * 1024 * 1024

WIN_HEAD_ORDER = (0, 4, 1, 5, 2, 6, 3, 7)

_F32 = jnp.float32
_BF16 = jnp.bfloat16


def _alibi_slopes(n):
    return np.array([2.0 ** (-8.0 * (i + 1) / n) for i in range(n)], dtype=np.float32)


def _bias_table(side, dil, slopes):
    kw = Q_BLOCK + 2 * side
    i = np.arange(Q_BLOCK)[:, None]
    j = np.arange(kw)[None, :]
    out = np.empty((3, len(slopes), Q_BLOCK, kw), np.float32)
    for v in range(3):
        rel = np.abs(j - i - side * v)
        dist = rel.astype(np.float32) * np.float32(dil)
        for h, sl in enumerate(slopes):
            out[v, h] = np.where(rel <= side, -(np.float32(sl) * dist) * np.float32(LOG2E),
                                 np.float32(NEG))
    return out.reshape(3, len(slopes) * Q_BLOCK, kw)


def _rmsnorm(x, g):
    return x * lax.rsqrt(jnp.mean(x * x, axis=-1, keepdims=True) + EPS) * g


def _inproj_kernel(x_ref, g_ref, w_ref, q1_ref, k1_ref, v1_ref, q4_ref, k4_ref, v4_ref,
                   q16_ref, k16_ref, v16_ref, qb_ref, kb_ref, vb_ref, nat_ref, cls_ref):
    tm = x_ref.shape[0]
    h = _rmsnorm(x_ref[...], g_ref[...]).astype(_BF16)
    scale = HEAD_DIM ** -0.5 * LOG2E

    def project(col, width, mul):
        p = jnp.dot(h, w_ref[:, col:col + width], preferred_element_type=_F32)
        return p if mul is None else p * mul

    dilated = ((q1_ref, q4_ref, q16_ref, scale), (k1_ref, k4_ref, k16_ref, None),
               (v1_ref, v4_ref, v16_ref, None))
    for n, (o1_ref, o4_ref, o16_ref, mul) in enumerate(dilated):
        p = project(n * D_MIX, D_MIX, mul)
        o1_ref[...] = p.astype(_BF16)
        for j in range(D_MIX // LANES):
            cols = slice(j * LANES, (j + 1) * LANES)
            nat_ref[n, j] = p[:, cols]
            for c in range(4):
                rows = nat_ref[n, j, pl.ds(c, tm // 4, stride=4), :]
                o4_ref[c, :, cols] = rows.astype(_BF16)
                cls_ref[n, j, c] = rows
            for c in range(4):
                for b in range(4):
                    rows = cls_ref[n, j, c, pl.ds(b, tm // 16, stride=4), :]
                    o16_ref[4 * b + c, :, cols] = rows.astype(_BF16)
    col = 3 * D_MIX
    for ref, width, mul in ((qb_ref, D_MIX, scale), (kb_ref, D_KV_WIN, None), (vb_ref, D_KV_WIN, None)):
        ref[...] = project(col, width, mul).astype(_BF16)
        col += width


def _class_block(tm, dil, width, tiles_per_batch):
    return pl.BlockSpec((None, dil, tm // dil, width),
                        lambda i: (i // tiles_per_batch, 0, i % tiles_per_batch, 0))


def _inproj(x2, g, w, batch, seq):
    n = x2.shape[0]
    tm = TOK_TILE_PROJ
    tpb = seq // tm
    row = lambda width: pl.BlockSpec((tm, width), lambda i: (i, 0))
    flat = lambda width: jax.ShapeDtypeStruct((n, width), _BF16)
    cls = lambda dil: jax.ShapeDtypeStruct((batch, dil, seq // dil, D_MIX), _BF16)
    return pl.pallas_call(
        _inproj_kernel,
        grid=(n // tm,),
        in_specs=[row(D_MODEL),
                  pl.BlockSpec((1, D_MODEL), lambda i: (0, 0)),
                  pl.BlockSpec(w.shape, lambda i: (0, 0))],
        out_specs=([row(D_MIX)] * 3 + [_class_block(tm, 4, D_MIX, tpb)] * 3
                   + [_class_block(tm, 16, D_MIX, tpb)] * 3
                   + [row(D_MIX), row(D_KV_WIN), row(D_KV_WIN)]),
        out_shape=([flat(D_MIX)] * 3 + [cls(4)] * 3 + [cls(16)] * 3
                   + [flat(D_MIX), flat(D_KV_WIN), flat(D_KV_WIN)]),
        scratch_shapes=[pltpu.VMEM((3, D_MIX // LANES, tm, LANES), _F32),
                        pltpu.VMEM((3, D_MIX // LANES, 4, tm // 4, LANES), _F32)],
        compiler_params=pltpu.CompilerParams(dimension_semantics=("arbitrary",),
                                             vmem_limit_bytes=VMEM_LIMIT),
        name="inproj",
    )(x2, g, w)


def _block_window(t, tq, bi, side, length):
    kw = Q_BLOCK + 2 * side
    q0 = t * tq + bi * Q_BLOCK
    kstart = jnp.clip(q0 - side, 0, length - kw)
    var = lax.shift_right_logical(q0 - kstart, side.bit_length() - 1)
    return pl.multiple_of(bi * Q_BLOCK, Q_BLOCK), pl.multiple_of(kstart, side), var


def _split_heads(qp, lo):
    zero = jnp.zeros_like(qp)
    return jnp.concatenate([jnp.where(lo, qp, zero), jnp.where(lo, zero, qp)], axis=0)


def _with_ones(vp):
    return jnp.concatenate([vp, jnp.ones_like(vp)], axis=1)


def _banded_attn_kernel(*refs, side, length, tq, shared_kv, with_sink):
    if with_sink:
        sink_ref, q_ref, k_ref, v_ref, bias_ref, o_ref = refs
    else:
        q_ref, k_ref, v_ref, bias_ref, o_ref, st_ref = refs
    t = pl.program_id(2)
    kw = Q_BLOCK + 2 * side
    n_pairs = N_HEADS // 2
    blocks_per_class = tq // Q_BLOCK
    n_iter = q_ref.shape[0] * blocks_per_class
    lane = lax.broadcasted_iota(jnp.int32, (Q_BLOCK, LANES), 1)
    lo = lane < HEAD_DIM
    pair_cols = [slice(p * LANES, (p + 1) * LANES) for p in range(n_pairs)]
    kv_cols = [slice(0, LANES)] * n_pairs if shared_kv else pair_cols
    nt_dims = (((1,), (1,)), ((), ()))

    pair_groups = [list(range(n_pairs))] if shared_kv else [[n] for n in range(n_pairs)]

    def sink_rows(pairs):
        return jnp.concatenate([jnp.full((Q_BLOCK, LANES), sink_ref[h], _F32)
                                for h in range(2 * pairs[0], 2 * pairs[-1] + 2)], axis=0)

    def group_block(ci, r0, kstart, var, pairs, st):
        rows = slice(2 * pairs[0] * Q_BLOCK, 2 * (pairs[-1] + 1) * Q_BLOCK)
        n_rows = rows.stop - rows.start
        q = jnp.concatenate([_split_heads(q_ref[ci, pl.ds(r0, Q_BLOCK), pair_cols[n]], lo)
                             for n in pairs], axis=0)
        cols = kv_cols[pairs[0]]
        s = lax.dot_general(q, k_ref[ci, pl.ds(kstart, kw), cols], nt_dims,
                            preferred_element_type=_F32)
        s = s + bias_ref[var, rows, :]
        m = jnp.broadcast_to(jnp.max(s, axis=-1, keepdims=True), (n_rows, LANES))
        if with_sink:
            m = jnp.maximum(m, sink_rows(pairs))
        p = jnp.concatenate([jnp.exp2(s[:, c:c + LANES] - m) for c in range(0, kw, LANES)],
                            axis=1).astype(_BF16)
        ol = jnp.dot(p, _with_ones(v_ref[ci, pl.ds(kstart, kw), cols]), preferred_element_type=_F32)
        o = ol[:, :LANES]
        l = ol[:, LANES:]
        den = l + jnp.exp2(sink_rows(pairs) - m) if with_sink else l
        for k, n in enumerate(pairs):
            top, mid, end = 2 * k * Q_BLOCK, (2 * k + 1) * Q_BLOCK, (2 * k + 2) * Q_BLOCK
            o_pair = jnp.where(lo, o[top:mid], o[mid:end])
            den_pair = jnp.where(lo, den[top:mid], den[mid:end])
            o_ref[ci, pl.ds(r0, Q_BLOCK), pair_cols[n]] = (o_pair * (1.0 / den_pair)).astype(_BF16)
            if not with_sink:
                for h, hrows in ((2 * n, slice(top, mid)), (2 * n + 1, slice(mid, end))):
                    st = jnp.where(lane == h, m[hrows], st)
                    st = jnp.where(lane == N_HEADS + h, l[hrows], st)
        return st

    def body(i, carry):
        ci = i // blocks_per_class
        r0, kstart, var = _block_window(t, tq, i % blocks_per_class, side, length)
        st = jnp.zeros((Q_BLOCK, LANES), _F32)
        for pairs in pair_groups:
            st = group_block(ci, r0, kstart, var, pairs, st)
        if not with_sink:
            st_ref[ci, pl.ds(r0, Q_BLOCK), :] = st
        return carry

    lax.fori_loop(0, n_iter, body, 0, unroll=UNROLL_BLOCKS)


def _banded_attention(q, k, v, bias, side, sink=None):
    batch, classes, length, _ = q.shape
    tq = min(Q_TILE, length)
    group = min(classes, Q_TILE // tq)
    kw = Q_BLOCK + 2 * side
    with_sink = sink is not None
    tile = lambda width: pl.BlockSpec((None, group, tq, width), lambda b, r, t: (b, r, t, 0))
    kv_full = pl.BlockSpec((None, group, length, k.shape[-1]), lambda b, r, t: (b, r, 0, 0))
    in_specs = [tile(D_MIX), kv_full, kv_full, pl.BlockSpec(bias.shape, lambda b, r, t: (0, 0, 0))]
    out_specs = [tile(D_MIX)]
    out_shape = [jax.ShapeDtypeStruct(q.shape, _BF16)]
    args = [q, k, v, bias]
    if with_sink:
        in_specs.insert(0, pl.BlockSpec(memory_space=pltpu.SMEM))
        args.insert(0, sink)
    else:
        out_specs.append(tile(LANES))
        out_shape.append(jax.ShapeDtypeStruct((batch, classes, length, LANES), _F32))
    return pl.pallas_call(
        functools.partial(_banded_attn_kernel, side=side, length=length, tq=tq,
                          shared_kv=k.shape[-1] == LANES, with_sink=with_sink),
        grid=(batch, classes // group, length // tq),
        in_specs=in_specs,
        out_specs=out_specs,
        out_shape=out_shape,
        compiler_params=pltpu.CompilerParams(dimension_semantics=("arbitrary",) * 3,
                                             vmem_limit_bytes=VMEM_LIMIT),
        name="win_attn" if with_sink else f"dil_attn_c{classes}",
    )(*args)


def _to_token_order(src4_ref, src16_ref, nat4_ref, nat16_ref, cls_ref):
    n_tiles, tm, _ = nat4_ref.shape
    for j in range(n_tiles):
        cols = slice(j * LANES, (j + 1) * LANES)
        for c in range(4):
            nat4_ref[j, pl.ds(c, tm // 4, stride=4), :] = src4_ref[c, :, cols].astype(_F32)
            for b in range(4):
                cls_ref[j, c, pl.ds(b, tm // 16, stride=4), :] = (
                    src16_ref[4 * b + c, :, cols].astype(_F32))
        for c in range(4):
            nat16_ref[j, pl.ds(c, tm // 4, stride=4), :] = cls_ref[j, c]


def _lane_tiles(ref):
    return jnp.concatenate([ref[j] for j in range(ref.shape[0])], axis=-1)


def _mix_ffn_kernel(x_ref, o1_ref, o4_ref, o16_ref, s1_ref, s4_ref, s16_ref, yb_ref, e_ref,
                    gd_ref, gw_ref, wout_ref, gf_ref, wg_ref, wu_ref, wd_ref, gfin_ref,
                    out_ref, act_ref, on4_ref, on16_ref, ocls_ref, sn4_ref, sn16_ref, scls_ref,
                    *, final):
    tm = x_ref.shape[0]
    _to_token_order(o4_ref, o16_ref, on4_ref, on16_ref, ocls_ref)
    _to_token_order(s4_ref, s16_ref, sn4_ref, sn16_ref, scls_ref)
    lane = lax.broadcasted_iota(jnp.int32, (tm, LANES), 1)
    stats = [s1_ref[...], sn4_ref[0], sn16_ref[0]]
    outs = [o1_ref[...].astype(_F32), _lane_tiles(on4_ref), _lane_tiles(on16_ref)]
    big = jnp.maximum(jnp.maximum(stats[0], stats[1]), stats[2])
    weights = [jnp.exp2(st - big) * pltpu.roll(st, LANES - N_HEADS, axis=1) for st in stats]
    inv = 1.0 / (weights[0] + weights[1] + weights[2])
    ya = jnp.zeros((tm, D_MIX), _F32)
    for w, o in zip(weights, outs):
        c = jnp.where(lane < N_HEADS, w * inv, 0.0)
        c_hi = c.astype(_BF16)
        c_lo = (c - c_hi.astype(_F32)).astype(_BF16)
        spread = (jnp.dot(c_hi, e_ref[...], preferred_element_type=_F32)
                  + jnp.dot(c_lo, e_ref[...], preferred_element_type=_F32))
        ya = ya + spread * o
    ya = _rmsnorm(ya, gd_ref[...]).astype(_BF16)
    yb = _rmsnorm(yb_ref[...].astype(_F32), gw_ref[...]).astype(_BF16)
    x1 = (x_ref[...]
          + jnp.dot(ya, wout_ref[:D_MIX, :], preferred_element_type=_F32)
          + jnp.dot(yb, wout_ref[D_MIX:, :], preferred_element_type=_F32))
    h = _rmsnorm(x1, gf_ref[...]).astype(_BF16)
    for c0 in range(0, D_FF, FF_CHUNK):
        g = jnp.dot(h, wg_ref[:, c0:c0 + FF_CHUNK], preferred_element_type=_F32)
        u = jnp.dot(h, wu_ref[:, c0:c0 + FF_CHUNK], preferred_element_type=_F32)
        act_ref[:, c0:c0 + FF_CHUNK] = (g * jax.nn.sigmoid(g) * u).astype(_BF16)
    out = x1 + jnp.dot(act_ref[...], wd_ref[...], preferred_element_type=_F32)
    if final:
        out = _rmsnorm(out, gfin_ref[...])
    out_ref[...] = out


def _mix_ffn(x2, outs, stats, yb, spread, gd, gw, wout, gf, wg, wu, wd, gfin, seq, final):
    n = x2.shape[0]
    tm = TOK_TILE_FFN
    tpb = seq // tm
    row = lambda width: pl.BlockSpec((tm, width), lambda i: (i, 0))
    whole = lambda a: pl.BlockSpec(a.shape, lambda i: (0, 0), pipeline_mode=pl.Buffered(1))
    per_pattern = lambda width: [row(width), _class_block(tm, 4, width, tpb),
                                 _class_block(tm, 16, width, tpb)]
    nat = lambda width: pltpu.VMEM((width // LANES, tm, LANES), _F32)
    cls = lambda width: pltpu.VMEM((width // LANES, 4, tm // 4, LANES), _F32)
    return pl.pallas_call(
        functools.partial(_mix_ffn_kernel, final=final),
        grid=(n // tm,),
        in_specs=[row(D_MODEL), *per_pattern(D_MIX), *per_pattern(LANES), row(D_MIX),
                  whole(spread), whole(gd), whole(gw), whole(wout), whole(gf),
                  whole(wg), whole(wu), whole(wd), whole(gfin)],
        out_specs=row(D_MODEL),
        out_shape=jax.ShapeDtypeStruct((n, D_MODEL), _F32),
        scratch_shapes=[pltpu.VMEM((tm, D_FF), _BF16), nat(D_MIX), nat(D_MIX), cls(D_MIX),
                        nat(LANES), nat(LANES), cls(LANES)],
        compiler_params=pltpu.CompilerParams(dimension_semantics=("arbitrary",),
                                             vmem_limit_bytes=VMEM_LIMIT),
        name="mix_ffn",
    )(x2, *outs, *stats, yb, spread, gd, gw, wout, gf, wg, wu, wd, gfin)


def kernel(x, g_mix, w_in, g_out_dil, g_out_win, sink, w_out, g_ffn, w_gate, w_up, w_down, g_final):
    batch, seq, _ = x.shape
    depth = w_in.shape[0]
    slopes = _alibi_slopes(N_ALIBI)
    order = np.array(WIN_HEAD_ORDER)
    slopes_win = slopes[:N_HEADS][order]
    slopes_dil = slopes[N_HEADS:]
    head_cols = (order[:, None] * HEAD_DIM + np.arange(HEAD_DIM)[None, :]).reshape(-1)

    bias_dil = [jnp.asarray(_bias_table(w // (2 * d), d, slopes_dil)) for w, d in DIL_PATTERNS]
    bias_win = jnp.asarray(_bias_table(WIN_SIDE, 1, slopes_win))
    spread = np.zeros((LANES, D_MIX), np.float32)
    for h in range(N_HEADS):
        spread[h, h * HEAD_DIM:(h + 1) * HEAD_DIM] = 1.0
    spread = jnp.asarray(spread, _BF16)

    qb0 = 3 * D_MIX
    in_cols = np.concatenate([np.arange(qb0), qb0 + head_cols, np.arange(qb0 + D_MIX, w_in.shape[-1])])
    w_in_k = w_in[:, :, in_cols].astype(_BF16)
    out_rows = np.concatenate([np.arange(D_MIX), D_MIX + head_cols])
    w_out_k = w_out[:, out_rows, :].astype(_BF16)
    g_win_k = g_out_win[:, head_cols]
    sink_k = sink[:, order] * LOG2E
    w_gate_k, w_up_k, w_down_k = (w.astype(_BF16) for w in (w_gate, w_up, w_down))
    vec = lambda g: g.reshape(1, -1)

    x2 = x.reshape(batch * seq, D_MODEL)
    for i in range(depth):
        q1, k1, v1, q4, k4, v4, q16, k16, v16, qb, kb, vb = _inproj(
            x2, vec(g_mix[i]), w_in_k[i], batch, seq)
        as_class = lambda a: a.reshape(batch, 1, seq, a.shape[-1])
        qkv = ((as_class(q1), as_class(k1), as_class(v1)), (q4, k4, v4), (q16, k16, v16))
        outs, stats = [], []
        for (window, dil), (q, k, v), bias in zip(DIL_PATTERNS, qkv, bias_dil):
            o, st = _banded_attention(q, k, v, bias, window // (2 * dil))
            outs.append(o)
            stats.append(st)
        outs[0] = outs[0].reshape(batch * seq, D_MIX)
        stats[0] = stats[0].reshape(batch * seq, LANES)
        (yb,) = _banded_attention(as_class(qb), as_class(kb), as_class(vb), bias_win, WIN_SIDE,
                                  sink_k[i])
        x2 = _mix_ffn(x2, outs, stats, yb.reshape(batch * seq, D_MIX), spread, vec(g_out_dil[i]),
                      vec(g_win_k[i]), w_out_k[i], vec(g_ffn[i]), w_gate_k[i], w_up_k[i], w_down_k[i],
                      vec(g_final), seq, final=(i == depth - 1))
    return x2.reshape(batch, seq, D_MODEL)
```

```python
import functools
import math

import jax
import jax.numpy as jnp
import numpy as np
from jax import lax
from jax.experimental import pallas as pl
from jax.experimental.pallas import tpu as pltpu

D_MODEL = 1024
HEAD_DIM = 64
N_HEADS = 8
N_KV_WIN = 2
DIL_PATTERNS = ((128, 1), (512, 4), (2048, 16))
WIN_SIDE = 128
D_MIX = N_HEADS * HEAD_DIM
D_KV_WIN = N_KV_WIN * HEAD_DIM
D_FF = 2816
EPS = 1e-6
NEG = -1e30
N_ALIBI = 2 * N_HEADS
LOG2E = math.log2(math.e)

LANES = 128
FF_CHUNK = 256
Q_BLOCK = 128
Q_TILE = 1024
UNROLL_BLOCKS = 8
TOK_TILE_PROJ = 512
TOK_TILE_FFN = 512
VMEM_LIMIT = 56 * 1024 * 1024

WIN_HEAD_ORDER = (0, 4, 1, 5, 2, 6, 3, 7)

_F32 = jnp.float32
_BF16 = jnp.bfloat16


def _alibi_slopes(n):
    return np.array([2.0 ** (-8.0 * (i + 1) / n) for i in range(n)], dtype=np.float32)


def _bias_table(side, dil, slopes):
    kw = Q_BLOCK + 2 * side
    i = np.arange(Q_BLOCK)[:, None]
    j = np.arange(kw)[None, :]
    out = np.empty((3, len(slopes), Q_BLOCK, kw), np.float32)
    for v in range(3):
        rel = np.abs(j - i - side * v)
        dist = rel.astype(np.float32) * np.float32(dil)
        for h, sl in enumerate(slopes):
            out[v, h] = np.where(rel <= side, -(np.float32(sl) * dist) * np.float32(LOG2E),
                                 np.float32(NEG))
    return out.reshape(3, len(slopes) * Q_BLOCK, kw)


def _rmsnorm(x, g):
    return x * lax.rsqrt(jnp.mean(x * x, axis=-1, keepdims=True) + EPS) * g


def _regroup_heads(y, to_kernel_order):
    lo = lax.broadcasted_iota(jnp.int32, (y.shape[0], LANES), 1) < HEAD_DIM
    t = [y[:, j * LANES:(j + 1) * LANES] for j in range(D_MIX // LANES)]
    r = [pltpu.roll(x, HEAD_DIM, axis=1) for x in t]
    if to_kernel_order:
        out = [jnp.where(lo, t[0], r[2]), jnp.where(lo, r[0], t[2]),
               jnp.where(lo, t[1], r[3]), jnp.where(lo, r[1], t[3])]
    else:
        out = [jnp.where(lo, t[0], r[1]), jnp.where(lo, t[2], r[3]),
               jnp.where(lo, r[0], t[1]), jnp.where(lo, r[2], t[3])]
    return jnp.concatenate(out, axis=1)


def _inproj_kernel(x_ref, g_ref, w_ref, q1_ref, k1_ref, v1_ref, q4_ref, k4_ref, v4_ref,
                   q16_ref, k16_ref, v16_ref, qb_ref, kb_ref, vb_ref, nat_ref, cls_ref):
    tm = x_ref.shape[0]
    h = _rmsnorm(x_ref[...], g_ref[...]).astype(_BF16)
    scale = HEAD_DIM ** -0.5 * LOG2E

    def project(col, width, mul):
        p = jnp.dot(h, w_ref[:, col:col + width], preferred_element_type=_F32)
        return p if mul is None else p * mul

    dilated = ((q1_ref, q4_ref, q16_ref, scale), (k1_ref, k4_ref, k16_ref, None),
               (v1_ref, v4_ref, v16_ref, None))
    for n, (o1_ref, o4_ref, o16_ref, mul) in enumerate(dilated):
        p = project(n * D_MIX, D_MIX, mul)
        o1_ref[...] = p.astype(_BF16)
        for j in range(D_MIX // LANES):
            cols = slice(j * LANES, (j + 1) * LANES)
            nat_ref[n, j] = p[:, cols]
            for c in range(4):
                rows = nat_ref[n, j, pl.ds(c, tm // 4, stride=4), :]
                o4_ref[c, :, cols] = rows.astype(_BF16)
                cls_ref[n, j, c] = rows
            for c in range(4):
                for b in range(4):
                    rows = cls_ref[n, j, c, pl.ds(b, tm // 16, stride=4), :]
                    o16_ref[4 * b + c, :, cols] = rows.astype(_BF16)
    qb = project(3 * D_MIX, D_MIX, scale)
    qb_ref[...] = _regroup_heads(qb, to_kernel_order=True).astype(_BF16)
    kb_ref[...] = project(4 * D_MIX, D_KV_WIN, None).astype(_BF16)
    vb_ref[...] = project(4 * D_MIX + D_KV_WIN, D_KV_WIN, None).astype(_BF16)


def _class_block(tm, dil, width, tiles_per_batch):
    return pl.BlockSpec((None, dil, tm // dil, width),
                        lambda i: (i // tiles_per_batch, 0, i % tiles_per_batch, 0))


def _inproj(x2, g, w, batch, seq):
    n = x2.shape[0]
    tm = TOK_TILE_PROJ
    tpb = seq // tm
    row = lambda width: pl.BlockSpec((tm, width), lambda i: (i, 0))
    flat = lambda width: jax.ShapeDtypeStruct((n, width), _BF16)
    cls = lambda dil: jax.ShapeDtypeStruct((batch, dil, seq // dil, D_MIX), _BF16)
    return pl.pallas_call(
        _inproj_kernel,
        grid=(n // tm,),
        in_specs=[row(D_MODEL),
                  pl.BlockSpec((1, D_MODEL), lambda i: (0, 0)),
                  pl.BlockSpec(w.shape, lambda i: (0, 0))],
        out_specs=([row(D_MIX)] * 3 + [_class_block(tm, 4, D_MIX, tpb)] * 3
                   + [_class_block(tm, 16, D_MIX, tpb)] * 3
                   + [row(D_MIX), row(D_KV_WIN), row(D_KV_WIN)]),
        out_shape=([flat(D_MIX)] * 3 + [cls(4)] * 3 + [cls(16)] * 3
                   + [flat(D_MIX), flat(D_KV_WIN), flat(D_KV_WIN)]),
        scratch_shapes=[pltpu.VMEM((3, D_MIX // LANES, tm, LANES), _F32),
                        pltpu.VMEM((3, D_MIX // LANES, 4, tm // 4, LANES), _F32)],
        compiler_params=pltpu.CompilerParams(dimension_semantics=("arbitrary",),
                                             vmem_limit_bytes=VMEM_LIMIT),
        name="inproj",
    )(x2, g, w)


def _block_window(t, tq, bi, side, length):
    kw = Q_BLOCK + 2 * side
    q0 = t * tq + bi * Q_BLOCK
    kstart = jnp.clip(q0 - side, 0, length - kw)
    var = lax.shift_right_logical(q0 - kstart, side.bit_length() - 1)
    return pl.multiple_of(bi * Q_BLOCK, Q_BLOCK), pl.multiple_of(kstart, side), var


def _split_heads(qp, lo):
    zero = jnp.zeros_like(qp)
    return jnp.concatenate([jnp.where(lo, qp, zero), jnp.where(lo, zero, qp)], axis=0)


def _with_ones(vp):
    return jnp.concatenate([vp, jnp.ones_like(vp)], axis=1)


def _banded_attn_kernel(*refs, side, length, tq, shared_kv, with_sink):
    if with_sink:
        sink_ref, q_ref, k_ref, v_ref, bias_ref, o_ref = refs
    else:
        q_ref, k_ref, v_ref, bias_ref, o_ref, st_ref = refs
    t = pl.program_id(2)
    kw = Q_BLOCK + 2 * side
    n_pairs = N_HEADS // 2
    blocks_per_class = tq // Q_BLOCK
    n_iter = q_ref.shape[0] * blocks_per_class
    lane = lax.broadcasted_iota(jnp.int32, (Q_BLOCK, LANES), 1)
    lo = lane < HEAD_DIM
    pair_cols = [slice(p * LANES, (p + 1) * LANES) for p in range(n_pairs)]
    kv_cols = [slice(0, LANES)] * n_pairs if shared_kv else pair_cols
    nt_dims = (((1,), (1,)), ((), ()))

    pair_groups = [list(range(n_pairs))] if shared_kv else [[n] for n in range(n_pairs)]

    def sink_rows(pairs):
        return jnp.concatenate([jnp.full((Q_BLOCK, LANES), sink_ref[WIN_HEAD_ORDER[h]], _F32)
                                for h in range(2 * pairs[0], 2 * pairs[-1] + 2)], axis=0)

    def group_block(ci, r0, kstart, var, pairs, st):
        rows = slice(2 * pairs[0] * Q_BLOCK, 2 * (pairs[-1] + 1) * Q_BLOCK)
        n_rows = rows.stop - rows.start
        q = jnp.concatenate([_split_heads(q_ref[ci, pl.ds(r0, Q_BLOCK), pair_cols[n]], lo)
                             for n in pairs], axis=0)
        cols = kv_cols[pairs[0]]
        s = lax.dot_general(q, k_ref[ci, pl.ds(kstart, kw), cols], nt_dims,
                            preferred_element_type=_F32)
        s = s + bias_ref[var, rows, :]
        m = jnp.broadcast_to(jnp.max(s, axis=-1, keepdims=True), (n_rows, LANES))
        if with_sink:
            m = jnp.maximum(m, sink_rows(pairs))
        p = jnp.concatenate([jnp.exp2(s[:, c:c + LANES] - m) for c in range(0, kw, LANES)],
                            axis=1).astype(_BF16)
        ol = jnp.dot(p, _with_ones(v_ref[ci, pl.ds(kstart, kw), cols]), preferred_element_type=_F32)
        o = ol[:, :LANES]
        l = ol[:, LANES:]
        den = l + jnp.exp2(sink_rows(pairs) - m) if with_sink else l
        for k, n in enumerate(pairs):
            top, mid, end = 2 * k * Q_BLOCK, (2 * k + 1) * Q_BLOCK, (2 * k + 2) * Q_BLOCK
            o_pair = jnp.where(lo, o[top:mid], o[mid:end])
            den_pair = jnp.where(lo, den[top:mid], den[mid:end])
            o_ref[ci, pl.ds(r0, Q_BLOCK), pair_cols[n]] = (o_pair * (1.0 / den_pair)).astype(_BF16)
            if not with_sink:
                for h, hrows in ((2 * n, slice(top, mid)), (2 * n + 1, slice(mid, end))):
                    st = jnp.where(lane == h, m[hrows], st)
                    st = jnp.where(lane == N_HEADS + h, l[hrows], st)
        return st

    def body(i, carry):
        ci = i // blocks_per_class
        r0, kstart, var = _block_window(t, tq, i % blocks_per_class, side, length)
        st = jnp.zeros((Q_BLOCK, LANES), _F32)
        for pairs in pair_groups:
            st = group_block(ci, r0, kstart, var, pairs, st)
        if not with_sink:
            st_ref[ci, pl.ds(r0, Q_BLOCK), :] = st
        return carry

    lax.fori_loop(0, n_iter, body, 0, unroll=UNROLL_BLOCKS)


def _banded_attention(q, k, v, bias, side, sink=None):
    batch, classes, length, _ = q.shape
    tq = min(Q_TILE, length)
    group = min(classes, Q_TILE // tq)
    kw = Q_BLOCK + 2 * side
    with_sink = sink is not None
    tile = lambda width: pl.BlockSpec((None, group, tq, width), lambda b, r, t: (b, r, t, 0))
    kv_full = pl.BlockSpec((None, group, length, k.shape[-1]), lambda b, r, t: (b, r, 0, 0))
    in_specs = [tile(D_MIX), kv_full, kv_full, pl.BlockSpec(bias.shape, lambda b, r, t: (0, 0, 0))]
    out_specs = [tile(D_MIX)]
    out_shape = [jax.ShapeDtypeStruct(q.shape, _BF16)]
    args = [q, k, v, bias]
    if with_sink:
        in_specs.insert(0, pl.BlockSpec(memory_space=pltpu.SMEM))
        args.insert(0, sink)
    else:
        out_specs.append(tile(LANES))
        out_shape.append(jax.ShapeDtypeStruct((batch, classes, length, LANES), _F32))
    return pl.pallas_call(
        functools.partial(_banded_attn_kernel, side=side, length=length, tq=tq,
                          shared_kv=k.shape[-1] == LANES, with_sink=with_sink),
        grid=(batch, classes // group, length // tq),
        in_specs=in_specs,
        out_specs=out_specs,
        out_shape=out_shape,
        compiler_params=pltpu.CompilerParams(dimension_semantics=("arbitrary",) * 3,
                                             vmem_limit_bytes=VMEM_LIMIT),
        name="win_attn" if with_sink else f"dil_attn_c{classes}",
    )(*args)


def _to_token_order(src4_ref, src16_ref, nat4_ref, nat16_ref, cls_ref):
    n_tiles, tm, _ = nat4_ref.shape
    for j in range(n_tiles):
        cols = slice(j * LANES, (j + 1) * LANES)
        for c in range(4):
            nat4_ref[j, pl.ds(c, tm // 4, stride=4), :] = src4_ref[c, :, cols].astype(_F32)
            for b in range(4):
                cls_ref[j, c, pl.ds(b, tm // 16, stride=4), :] = (
                    src16_ref[4 * b + c, :, cols].astype(_F32))
        for c in range(4):
            nat16_ref[j, pl.ds(c, tm // 4, stride=4), :] = cls_ref[j, c]


def _lane_tiles(ref):
    return jnp.concatenate([ref[j] for j in range(ref.shape[0])], axis=-1)


def _mix_ffn_kernel(x_ref, o1_ref, o4_ref, o16_ref, s1_ref, s4_ref, s16_ref, yb_ref, e_ref,
                    gd_ref, gw_ref, wout_ref, gf_ref, wg_ref, wu_ref, wd_ref, gfin_ref,
                    out_ref, act_ref, on4_ref, on16_ref, ocls_ref, sn4_ref, sn16_ref, scls_ref,
                    *, final):
    tm = x_ref.shape[0]
    _to_token_order(o4_ref, o16_ref, on4_ref, on16_ref, ocls_ref)
    _to_token_order(s4_ref, s16_ref, sn4_ref, sn16_ref, scls_ref)
    lane = lax.broadcasted_iota(jnp.int32, (tm, LANES), 1)
    stats = [s1_ref[...], sn4_ref[0], sn16_ref[0]]
    outs = [o1_ref[...].astype(_F32), _lane_tiles(on4_ref), _lane_tiles(on16_ref)]
    big = jnp.maximum(jnp.maximum(stats[0], stats[1]), stats[2])
    weights = [jnp.exp2(st - big) * pltpu.roll(st, LANES - N_HEADS, axis=1) for st in stats]
    inv = 1.0 / (weights[0] + weights[1] + weights[2])
    ya = jnp.zeros((tm, D_MIX), _F32)
    for w, o in zip(weights, outs):
        c = jnp.where(lane < N_HEADS, w * inv, 0.0)
        c_hi = c.astype(_BF16)
        c_lo = (c - c_hi.astype(_F32)).astype(_BF16)
        spread = jnp.dot(jnp.concatenate([c_hi, c_lo], axis=1), e_ref[...],
                         preferred_element_type=_F32)
        ya = ya + spread * o
    ya = _rmsnorm(ya, gd_ref[...]).astype(_BF16)
    yb = _regroup_heads(yb_ref[...].astype(_F32), to_kernel_order=False)
    yb = _rmsnorm(yb, gw_ref[...]).astype(_BF16)
    x1 = (x_ref[...]
          + jnp.dot(ya, wout_ref[:D_MIX, :], preferred_element_type=_F32)
          + jnp.dot(yb, wout_ref[D_MIX:, :], preferred_element_type=_F32))
    h = _rmsnorm(x1, gf_ref[...]).astype(_BF16)
    for c0 in range(0, D_FF, FF_CHUNK):
        g = jnp.dot(h, wg_ref[:, c0:c0 + FF_CHUNK], preferred_element_type=_F32)
        u = jnp.dot(h, wu_ref[:, c0:c0 + FF_CHUNK], preferred_element_type=_F32)
        act_ref[:, c0:c0 + FF_CHUNK] = (g * jax.nn.sigmoid(g) * u).astype(_BF16)
    out = x1 + jnp.dot(act_ref[...], wd_ref[...], preferred_element_type=_F32)
    if final:
        out = _rmsnorm(out, gfin_ref[...])
    out_ref[...] = out


def _mix_ffn(x2, outs, stats, yb, spread, gd, gw, wout, gf, wg, wu, wd, gfin, seq, final):
    n = x2.shape[0]
    tm = TOK_TILE_FFN
    tpb = seq // tm
    row = lambda width: pl.BlockSpec((tm, width), lambda i: (i, 0))
    whole = lambda a: pl.BlockSpec(a.shape, lambda i: (0, 0), pipeline_mode=pl.Buffered(1))
    per_pattern = lambda width: [row(width), _class_block(tm, 4, width, tpb),
                                 _class_block(tm, 16, width, tpb)]
    nat = lambda width: pltpu.VMEM((width // LANES, tm, LANES), _F32)
    cls = lambda width: pltpu.VMEM((width // LANES, 4, tm // 4, LANES), _F32)
    return pl.pallas_call(
        functools.partial(_mix_ffn_kernel, final=final),
        grid=(n // tm,),
        in_specs=[row(D_MODEL), *per_pattern(D_MIX), *per_pattern(LANES), row(D_MIX),
                  whole(spread), whole(gd), whole(gw), whole(wout), whole(gf),
                  whole(wg), whole(wu), whole(wd), whole(gfin)],
        out_specs=row(D_MODEL),
        out_shape=jax.ShapeDtypeStruct((n, D_MODEL), _F32),
        scratch_shapes=[pltpu.VMEM((tm, D_FF), _BF16), nat(D_MIX), nat(D_MIX), cls(D_MIX),
                        nat(LANES), nat(LANES), cls(LANES)],
        compiler_params=pltpu.CompilerParams(dimension_semantics=("arbitrary",),
                                             vmem_limit_bytes=VMEM_LIMIT),
        name="mix_ffn",
    )(x2, *outs, *stats, yb, spread, gd, gw, wout, gf, wg, wu, wd, gfin)


def kernel(x, g_mix, w_in, g_out_dil, g_out_win, sink, w_out, g_ffn, w_gate, w_up, w_down, g_final):
    batch, seq, _ = x.shape
    depth = w_in.shape[0]
    slopes = _alibi_slopes(N_ALIBI)
    slopes_win = slopes[:N_HEADS][np.array(WIN_HEAD_ORDER)]
    slopes_dil = slopes[N_HEADS:]
    bias_dil = [jnp.asarray(_bias_table(w // (2 * d), d, slopes_dil)) for w, d in DIL_PATTERNS]
    bias_win = jnp.asarray(_bias_table(WIN_SIDE, 1, slopes_win))
    spread = np.zeros((2, LANES, D_MIX), np.float32)
    for h in range(N_HEADS):
        spread[:, h, h * HEAD_DIM:(h + 1) * HEAD_DIM] = 1.0
    spread = jnp.asarray(spread.reshape(2 * LANES, D_MIX), _BF16)

    w_in_k, w_out_k, w_gate_k, w_up_k, w_down_k = (
        w.astype(_BF16) for w in (w_in, w_out, w_gate, w_up, w_down))
    sink_k = sink * LOG2E
    vec = lambda g: g.reshape(1, -1)

    x2 = x.reshape(batch * seq, D_MODEL)
    for i in range(depth):
        q1, k1, v1, q4, k4, v4, q16, k16, v16, qb, kb, vb = _inproj(
            x2, vec(g_mix[i]), w_in_k[i], batch, seq)
        as_class = lambda a: a.reshape(batch, 1, seq, a.shape[-1])
        qkv = ((as_class(q1), as_class(k1), as_class(v1)), (q4, k4, v4), (q16, k16, v16))
        outs, stats = [], []
        for (window, dil), (q, k, v), bias in zip(DIL_PATTERNS, qkv, bias_dil):
            o, st = _banded_attention(q, k, v, bias, window // (2 * dil))
            outs.append(o)
            stats.append(st)
        outs[0] = outs[0].reshape(batch * seq, D_MIX)
        stats[0] = stats[0].reshape(batch * seq, LANES)
        (yb,) = _banded_attention(as_class(qb), as_class(kb), as_class(vb), bias_win, WIN_SIDE,
                                  sink_k[i])
        x2 = _mix_ffn(x2, outs, stats, yb.reshape(batch * seq, D_MIX), spread, vec(g_out_dil[i]),
                      vec(g_out_win[i]), w_out_k[i], vec(g_ffn[i]), w_gate_k[i], w_up_k[i], w_down_k[i],
                      vec(g_final), seq, final=(i == depth - 1))
    return x2.reshape(batch, seq, D_MODEL)
```

```python
import functools
import math

import jax
import jax.numpy as jnp
import numpy as np
from jax import lax
from jax.experimental import pallas as pl
from jax.experimental.pallas import tpu as pltpu

D_MODEL = 1024
HEAD_DIM = 64
N_HEADS = 8
N_KV_WIN = 2
DIL_PATTERNS = ((128, 1), (512, 4), (2048, 16))
WIN_SIDE = 128
D_MIX = N_HEADS * HEAD_DIM
D_KV_WIN = N_KV_WIN * HEAD_DIM
D_FF = 2816
EPS = 1e-6
NEG = -1e30
N_ALIBI = 2 * N_HEADS
LOG2E = math.log2(math.e)

LANES = 128
FF_CHUNK = 256
Q_BLOCK = 128
Q_TILE = 1024
UNROLL_BLOCKS = 8
TOK_TILE_PROJ = 1024
TOK_TILE_FFN = 512
VMEM_LIMIT = 56 * 1024 * 1024

WIN_HEAD_ORDER = (0, 4, 1, 5, 2, 6, 3, 7)

_F32 = jnp.float32
_BF16 = jnp.bfloat16


def _alibi_slopes(n):
    return np.array([2.0 ** (-8.0 * (i + 1) / n) for i in range(n)], dtype=np.float32)


def _bias_table(side, dil, slopes):
    kw = Q_BLOCK + 2 * side
    i = np.arange(Q_BLOCK)[:, None]
    j = np.arange(kw)[None, :]
    out = np.empty((3, len(slopes), Q_BLOCK, kw), np.float32)
    for v in range(3):
        rel = np.abs(j - i - side * v)
        dist = rel.astype(np.float32) * np.float32(dil)
        for h, sl in enumerate(slopes):
            out[v, h] = np.where(rel <= side, -(np.float32(sl) * dist) * np.float32(LOG2E),
                                 np.float32(NEG))
    return out.reshape(3, len(slopes) * Q_BLOCK, kw)


def _rmsnorm(x, g):
    return x * lax.rsqrt(jnp.mean(x * x, axis=-1, keepdims=True) + EPS) * g


def _regroup_heads(y, to_kernel_order):
    lo = lax.broadcasted_iota(jnp.int32, (y.shape[0], LANES), 1) < HEAD_DIM
    t = [y[:, j * LANES:(j + 1) * LANES] for j in range(D_MIX // LANES)]
    r = [pltpu.roll(x, HEAD_DIM, axis=1) for x in t]
    if to_kernel_order:
        out = [jnp.where(lo, t[0], r[2]), jnp.where(lo, r[0], t[2]),
               jnp.where(lo, t[1], r[3]), jnp.where(lo, r[1], t[3])]
    else:
        out = [jnp.where(lo, t[0], r[1]), jnp.where(lo, t[2], r[3]),
               jnp.where(lo, r[0], t[1]), jnp.where(lo, r[2], t[3])]
    return jnp.concatenate(out, axis=1)


def _inproj_kernel(x_ref, g_ref, w_ref, q1_ref, k1_ref, v1_ref, q4_ref, k4_ref, v4_ref,
                   q16_ref, k16_ref, v16_ref, qb_ref, kb_ref, vb_ref, nat_ref, cls_ref):
    tm = x_ref.shape[0]
    h = _rmsnorm(x_ref[...], g_ref[...]).astype(_BF16)
    scale = HEAD_DIM ** -0.5 * LOG2E

    def project(col, width, mul):
        p = jnp.dot(h, w_ref[:, col:col + width], preferred_element_type=_F32)
        return p if mul is None else p * mul

    dilated = ((q1_ref, q4_ref, q16_ref, scale), (k1_ref, k4_ref, k16_ref, None),
               (v1_ref, v4_ref, v16_ref, None))
    for n, (o1_ref, o4_ref, o16_ref, mul) in enumerate(dilated):
        p = project(n * D_MIX, D_MIX, mul)
        o1_ref[...] = p.astype(_BF16)
        for j in range(D_MIX // LANES):
            cols = slice(j * LANES, (j + 1) * LANES)
            nat_ref[n, j] = p[:, cols]
            for c in range(4):
                rows = nat_ref[n, j, pl.ds(c, tm // 4, stride=4), :]
                o4_ref[c, :, cols] = rows.astype(_BF16)
                cls_ref[n, j, c] = rows
            for c in range(4):
                for b in range(4):
                    rows = cls_ref[n, j, c, pl.ds(b, tm // 16, stride=4), :]
                    o16_ref[4 * b + c, :, cols] = rows.astype(_BF16)
    qb = project(3 * D_MIX, D_MIX, scale)
    qb_ref[...] = _regroup_heads(qb, to_kernel_order=True).astype(_BF16)
    kb_ref[...] = project(4 * D_MIX, D_KV_WIN, None).astype(_BF16)
    vb_ref[...] = project(4 * D_MIX + D_KV_WIN, D_KV_WIN, None).astype(_BF16)


def _class_block(tm, dil, width, tiles_per_batch):
    return pl.BlockSpec((None, dil, tm // dil, width),
                        lambda i: (i // tiles_per_batch, 0, i % tiles_per_batch, 0))


def _layer_block(stacked, layer):
    return pl.BlockSpec((None,) + stacked.shape[1:], lambda i: (layer, 0, 0),
                        pipeline_mode=pl.Buffered(1))


def _inproj(x2, g, w, layer, batch, seq):
    n = x2.shape[0]
    tm = TOK_TILE_PROJ
    tpb = seq // tm
    row = lambda width: pl.BlockSpec((tm, width), lambda i: (i, 0))
    flat = lambda width: jax.ShapeDtypeStruct((n, width), _BF16)
    cls = lambda dil: jax.ShapeDtypeStruct((batch, dil, seq // dil, D_MIX), _BF16)
    return pl.pallas_call(
        _inproj_kernel,
        grid=(n // tm,),
        in_specs=[row(D_MODEL),
                  _layer_block(g, layer), _layer_block(w, layer)],
        out_specs=([row(D_MIX)] * 3 + [_class_block(tm, 4, D_MIX, tpb)] * 3
                   + [_class_block(tm, 16, D_MIX, tpb)] * 3
                   + [row(D_MIX), row(D_KV_WIN), row(D_KV_WIN)]),
        out_shape=([flat(D_MIX)] * 3 + [cls(4)] * 3 + [cls(16)] * 3
                   + [flat(D_MIX), flat(D_KV_WIN), flat(D_KV_WIN)]),
        scratch_shapes=[pltpu.VMEM((3, D_MIX // LANES, tm, LANES), _F32),
                        pltpu.VMEM((3, D_MIX // LANES, 4, tm // 4, LANES), _F32)],
        compiler_params=pltpu.CompilerParams(dimension_semantics=("arbitrary",),
                                             vmem_limit_bytes=VMEM_LIMIT),
        name="inproj",
    )(x2, g, w)


def _block_window(t, tq, bi, side, length):
    kw = Q_BLOCK + 2 * side
    q0 = t * tq + bi * Q_BLOCK
    kstart = jnp.clip(q0 - side, 0, length - kw)
    var = lax.shift_right_logical(q0 - kstart, side.bit_length() - 1)
    return pl.multiple_of(bi * Q_BLOCK, Q_BLOCK), pl.multiple_of(kstart, side), var


def _split_heads(qp, lo):
    zero = jnp.zeros_like(qp)
    return jnp.concatenate([jnp.where(lo, qp, zero), jnp.where(lo, zero, qp)], axis=0)


def _with_ones(vp):
    return jnp.concatenate([vp, jnp.ones_like(vp)], axis=1)


def _banded_attn_kernel(*refs, side, length, tq, shared_kv, with_sink, layer):
    if with_sink:
        sink_ref, q_ref, k_ref, v_ref, bias_ref, o_ref = refs
    else:
        q_ref, k_ref, v_ref, bias_ref, o_ref, st_ref = refs
    t = pl.program_id(2)
    kw = Q_BLOCK + 2 * side
    n_pairs = N_HEADS // 2
    blocks_per_class = tq // Q_BLOCK
    n_iter = q_ref.shape[0] * blocks_per_class
    lane = lax.broadcasted_iota(jnp.int32, (Q_BLOCK, LANES), 1)
    lo = lane < HEAD_DIM
    pair_cols = [slice(p * LANES, (p + 1) * LANES) for p in range(n_pairs)]
    kv_cols = [slice(0, LANES)] * n_pairs if shared_kv else pair_cols
    nt_dims = (((1,), (1,)), ((), ()))

    pair_groups = [list(range(n_pairs))] if shared_kv else [[n] for n in range(n_pairs)]

    def sink_rows(pairs):
        return jnp.concatenate([jnp.full((Q_BLOCK, LANES), sink_ref[layer, WIN_HEAD_ORDER[h]], _F32)
                                for h in range(2 * pairs[0], 2 * pairs[-1] + 2)], axis=0)

    def group_block(ci, r0, kstart, var, pairs, st):
        rows = slice(2 * pairs[0] * Q_BLOCK, 2 * (pairs[-1] + 1) * Q_BLOCK)
        n_rows = rows.stop - rows.start
        q = jnp.concatenate([_split_heads(q_ref[ci, pl.ds(r0, Q_BLOCK), pair_cols[n]], lo)
                             for n in pairs], axis=0)
        cols = kv_cols[pairs[0]]
        s = lax.dot_general(q, k_ref[ci, pl.ds(kstart, kw), cols], nt_dims,
                            preferred_element_type=_F32)
        s = s + bias_ref[var, rows, :]
        m = jnp.broadcast_to(jnp.max(s, axis=-1, keepdims=True), (n_rows, LANES))
        if with_sink:
            m = jnp.maximum(m, sink_rows(pairs))
        p = jnp.concatenate([jnp.exp2(s[:, c:c + LANES] - m) for c in range(0, kw, LANES)],
                            axis=1).astype(_BF16)
        ol = jnp.dot(p, _with_ones(v_ref[ci, pl.ds(kstart, kw), cols]), preferred_element_type=_F32)
        o = ol[:, :LANES]
        l = ol[:, LANES:]
        den = l + jnp.exp2(sink_rows(pairs) - m) if with_sink else l
        for k, n in enumerate(pairs):
            top, mid, end = 2 * k * Q_BLOCK, (2 * k + 1) * Q_BLOCK, (2 * k + 2) * Q_BLOCK
            o_pair = jnp.where(lo, o[top:mid], o[mid:end])
            den_pair = jnp.where(lo, den[top:mid], den[mid:end])
            o_ref[ci, pl.ds(r0, Q_BLOCK), pair_cols[n]] = (o_pair * (1.0 / den_pair)).astype(_BF16)
            if not with_sink:
                for h, hrows in ((2 * n, slice(top, mid)), (2 * n + 1, slice(mid, end))):
                    st = jnp.where(lane == h, m[hrows], st)
                    st = jnp.where(lane == N_HEADS + h, l[hrows], st)
        return st

    def body(i, carry):
        ci = i // blocks_per_class
        r0, kstart, var = _block_window(t, tq, i % blocks_per_class, side, length)
        st = jnp.zeros((Q_BLOCK, LANES), _F32)
        for pairs in pair_groups:
            st = group_block(ci, r0, kstart, var, pairs, st)
        if not with_sink:
            st_ref[ci, pl.ds(r0, Q_BLOCK), :] = st
        return carry

    lax.fori_loop(0, n_iter, body, 0, unroll=UNROLL_BLOCKS)


def _banded_attention(q, k, v, bias, side, sink=None, layer=0):
    batch, classes, length, _ = q.shape
    tq = min(Q_TILE, length)
    group = min(classes, Q_TILE // tq)
    kw = Q_BLOCK + 2 * side
    with_sink = sink is not None
    tile = lambda width: pl.BlockSpec((None, group, tq, width), lambda b, r, t: (b, r, t, 0))
    kv_full = pl.BlockSpec((None, group, length, k.shape[-1]), lambda b, r, t: (b, r, 0, 0))
    in_specs = [tile(D_MIX), kv_full, kv_full, pl.BlockSpec(bias.shape, lambda b, r, t: (0, 0, 0))]
    out_specs = [tile(D_MIX)]
    out_shape = [jax.ShapeDtypeStruct(q.shape, _BF16)]
    args = [q, k, v, bias]
    if with_sink:
        in_specs.insert(0, pl.BlockSpec(memory_space=pltpu.SMEM))
        args.insert(0, sink)
    else:
        out_specs.append(tile(LANES))
        out_shape.append(jax.ShapeDtypeStruct((batch, classes, length, LANES), _F32))
    return pl.pallas_call(
        functools.partial(_banded_attn_kernel, side=side, length=length, tq=tq,
                          shared_kv=k.shape[-1] == LANES, with_sink=with_sink, layer=layer),
        grid=(batch, classes // group, length // tq),
        in_specs=in_specs,
        out_specs=out_specs,
        out_shape=out_shape,
        compiler_params=pltpu.CompilerParams(dimension_semantics=("arbitrary",) * 3,
                                             vmem_limit_bytes=VMEM_LIMIT),
        name="win_attn" if with_sink else f"dil_attn_c{classes}",
    )(*args)


def _to_token_order(src4_ref, src16_ref, nat4_ref, nat16_ref, cls_ref):
    n_tiles, tm, _ = nat4_ref.shape
    for j in range(n_tiles):
        cols = slice(j * LANES, (j + 1) * LANES)
        for c in range(4):
            nat4_ref[j, pl.ds(c, tm // 4, stride=4), :] = src4_ref[c, :, cols].astype(_F32)
            for b in range(4):
                cls_ref[j, c, pl.ds(b, tm // 16, stride=4), :] = (
                    src16_ref[4 * b + c, :, cols].astype(_F32))
        for c in range(4):
            nat16_ref[j, pl.ds(c, tm // 4, stride=4), :] = cls_ref[j, c]


def _lane_tiles(ref):
    return jnp.concatenate([ref[j] for j in range(ref.shape[0])], axis=-1)


def _mix_ffn_kernel(x_ref, o1_ref, o4_ref, o16_ref, s1_ref, s4_ref, s16_ref, yb_ref, e_ref,
                    gd_ref, gw_ref, wout_ref, gf_ref, wg_ref, wu_ref, wd_ref, gfin_ref,
                    out_ref, act_ref, on4_ref, on16_ref, ocls_ref, sn4_ref, sn16_ref, scls_ref,
                    *, final):
    tm = x_ref.shape[0]
    _to_token_order(o4_ref, o16_ref, on4_ref, on16_ref, ocls_ref)
    _to_token_order(s4_ref, s16_ref, sn4_ref, sn16_ref, scls_ref)
    lane = lax.broadcasted_iota(jnp.int32, (tm, LANES), 1)
    stats = [s1_ref[...], sn4_ref[0], sn16_ref[0]]
    outs = [o1_ref[...].astype(_F32), _lane_tiles(on4_ref), _lane_tiles(on16_ref)]
    big = jnp.maximum(jnp.maximum(stats[0], stats[1]), stats[2])
    weights = [jnp.exp2(st - big) * pltpu.roll(st, LANES - N_HEADS, axis=1) for st in stats]
    inv = 1.0 / (weights[0] + weights[1] + weights[2])
    ya = jnp.zeros((tm, D_MIX), _F32)
    for w, o in zip(weights, outs):
        c = jnp.where(lane < N_HEADS, w * inv, 0.0)
        c_hi = c.astype(_BF16)
        c_lo = (c - c_hi.astype(_F32)).astype(_BF16)
        spread = jnp.dot(jnp.concatenate([c_hi, c_lo], axis=1), e_ref[...],
                         preferred_element_type=_F32)
        ya = ya + spread * o
    ya = _rmsnorm(ya, gd_ref[...]).astype(_BF16)
    yb = _regroup_heads(yb_ref[...].astype(_F32), to_kernel_order=False)
    yb = _rmsnorm(yb, gw_ref[...]).astype(_BF16)
    x1 = (x_ref[...]
          + jnp.dot(ya, wout_ref[:D_MIX, :], preferred_element_type=_F32)
          + jnp.dot(yb, wout_ref[D_MIX:, :], preferred_element_type=_F32))
    h = _rmsnorm(x1, gf_ref[...]).astype(_BF16)
    for c0 in range(0, D_FF, FF_CHUNK):
        g = jnp.dot(h, wg_ref[:, c0:c0 + FF_CHUNK], preferred_element_type=_F32)
        u = jnp.dot(h, wu_ref[:, c0:c0 + FF_CHUNK], preferred_element_type=_F32)
        act_ref[:, c0:c0 + FF_CHUNK] = (g * jax.nn.sigmoid(g) * u).astype(_BF16)
    out = x1 + jnp.dot(act_ref[...], wd_ref[...], preferred_element_type=_F32)
    if final:
        out = _rmsnorm(out, gfin_ref[...])
    out_ref[...] = out


def _mix_ffn(x2, outs, stats, yb, spread, gd, gw, wout, gf, wg, wu, wd, gfin, layer, seq, final):
    n = x2.shape[0]
    tm = TOK_TILE_FFN
    tpb = seq // tm
    row = lambda width: pl.BlockSpec((tm, width), lambda i: (i, 0))
    whole = lambda a: pl.BlockSpec(a.shape, lambda i: (0, 0), pipeline_mode=pl.Buffered(1))
    per_pattern = lambda width: [row(width), _class_block(tm, 4, width, tpb),
                                 _class_block(tm, 16, width, tpb)]
    nat = lambda width: pltpu.VMEM((width // LANES, tm, LANES), _F32)
    cls = lambda width: pltpu.VMEM((width // LANES, 4, tm // 4, LANES), _F32)
    return pl.pallas_call(
        functools.partial(_mix_ffn_kernel, final=final),
        grid=(n // tm,),
        in_specs=[row(D_MODEL), *per_pattern(D_MIX), *per_pattern(LANES), row(D_MIX),
                  whole(spread), *[_layer_block(a, layer) for a in (gd, gw, wout, gf, wg, wu, wd)],
                  whole(gfin)],
        out_specs=row(D_MODEL),
        out_shape=jax.ShapeDtypeStruct((n, D_MODEL), _F32),
        scratch_shapes=[pltpu.VMEM((tm, D_FF), _BF16), nat(D_MIX), nat(D_MIX), cls(D_MIX),
                        nat(LANES), nat(LANES), cls(LANES)],
        compiler_params=pltpu.CompilerParams(dimension_semantics=("arbitrary",),
                                             vmem_limit_bytes=VMEM_LIMIT),
        name="mix_ffn",
    )(x2, *outs, *stats, yb, spread, gd, gw, wout, gf, wg, wu, wd, gfin)


def kernel(x, g_mix, w_in, g_out_dil, g_out_win, sink, w_out, g_ffn, w_gate, w_up, w_down, g_final):
    batch, seq, _ = x.shape
    depth = w_in.shape[0]
    slopes = _alibi_slopes(N_ALIBI)
    slopes_win = slopes[:N_HEADS][np.array(WIN_HEAD_ORDER)]
    slopes_dil = slopes[N_HEADS:]
    bias_dil = [jnp.asarray(_bias_table(w // (2 * d), d, slopes_dil)) for w, d in DIL_PATTERNS]
    bias_win = jnp.asarray(_bias_table(WIN_SIDE, 1, slopes_win))
    spread = np.zeros((2, LANES, D_MIX), np.float32)
    for h in range(N_HEADS):
        spread[:, h, h * HEAD_DIM:(h + 1) * HEAD_DIM] = 1.0
    spread = jnp.asarray(spread.reshape(2 * LANES, D_MIX), _BF16)

    w_in_k, w_out_k, w_gate_k, w_up_k, w_down_k = (
        w.astype(_BF16) for w in (w_in, w_out, w_gate, w_up, w_down))
    sink_k = sink * LOG2E
    vecs = lambda g: g.reshape(depth, 1, -1)

    x2 = x.reshape(batch * seq, D_MODEL)
    for i in range(depth):
        q1, k1, v1, q4, k4, v4, q16, k16, v16, qb, kb, vb = _inproj(
            x2, vecs(g_mix), w_in_k, i, batch, seq)
        as_class = lambda a: a.reshape(batch, 1, seq, a.shape[-1])
        qkv = ((as_class(q1), as_class(k1), as_class(v1)), (q4, k4, v4), (q16, k16, v16))
        outs, stats = [], []
        for (window, dil), (q, k, v), bias in zip(DIL_PATTERNS, qkv, bias_dil):
            o, st = _banded_attention(q, k, v, bias, window // (2 * dil))
            outs.append(o)
            stats.append(st)
        outs[0] = outs[0].reshape(batch * seq, D_MIX)
        stats[0] = stats[0].reshape(batch * seq, LANES)
        (yb,) = _banded_attention(as_class(qb), as_class(kb), as_class(vb), bias_win, WIN_SIDE,
                                  sink_k, i)
        x2 = _mix_ffn(x2, outs, stats, yb.reshape(batch * seq, D_MIX), spread, vecs(g_out_dil),
                      vecs(g_out_win), w_out_k, vecs(g_ffn), w_gate_k, w_up_k, w_down_k,
                      g_final.reshape(1, -1), i, seq, final=(i == depth - 1))
    return x2.reshape(batch, seq, D_MODEL)
```

```python
import functools
import math

import jax
import jax.numpy as jnp
import numpy as np
from jax import lax
from jax.experimental import pallas as pl
from jax.experimental.pallas import tpu as pltpu

D_MODEL = 1024
HEAD_DIM = 64
N_HEADS = 8
N_KV_WIN = 2
DIL_PATTERNS = ((128, 1), (512, 4), (2048, 16))
WIN_SIDE = 128
D_MIX = N_HEADS * HEAD_DIM
D_KV_WIN = N_KV_WIN * HEAD_DIM
D_FF = 2816
EPS = 1e-6
NEG = -1e30
N_ALIBI = 2 * N_HEADS
LOG2E = math.log2(math.e)

LANES = 128
FF_CHUNK = 256
Q_BLOCK = 128
Q_TILE = 2048
UNROLL_BLOCKS = 8
TOK_TILE_PROJ = 1024
TOK_TILE_FFN = 512
VMEM_LIMIT = 56 * 1024 * 1024

WIN_HEAD_ORDER = (0, 4, 1, 5, 2, 6, 3, 7)

_F32 = jnp.float32
_BF16 = jnp.bfloat16


def _alibi_slopes(n):
    return np.array([2.0 ** (-8.0 * (i + 1) / n) for i in range(n)], dtype=np.float32)


def _bias_table(side, dil, slopes):
    kw = Q_BLOCK + 2 * side
    i = np.arange(Q_BLOCK)[:, None]
    j = np.arange(kw)[None, :]
    out = np.empty((3, len(slopes), Q_BLOCK, kw), np.float32)
    for v in range(3):
        rel = np.abs(j - i - side * v)
        dist = rel.astype(np.float32) * np.float32(dil)
        for h, sl in enumerate(slopes):
            out[v, h] = np.where(rel <= side, -(np.float32(sl) * dist) * np.float32(LOG2E),
                                 np.float32(NEG))
    return out.reshape(3, len(slopes) * Q_BLOCK, kw)


def _rmsnorm(x, g):
    return x * lax.rsqrt(jnp.mean(x * x, axis=-1, keepdims=True) + EPS) * g


def _regroup_heads(y, to_kernel_order):
    lo = lax.broadcasted_iota(jnp.int32, (y.shape[0], LANES), 1) < HEAD_DIM
    t = [y[:, j * LANES:(j + 1) * LANES] for j in range(D_MIX // LANES)]
    r = [pltpu.roll(x, HEAD_DIM, axis=1) for x in t]
    if to_kernel_order:
        out = [jnp.where(lo, t[0], r[2]), jnp.where(lo, r[0], t[2]),
               jnp.where(lo, t[1], r[3]), jnp.where(lo, r[1], t[3])]
    else:
        out = [jnp.where(lo, t[0], r[1]), jnp.where(lo, t[2], r[3]),
               jnp.where(lo, r[0], t[1]), jnp.where(lo, r[2], t[3])]
    return jnp.concatenate(out, axis=1)


def _inproj_kernel(x_ref, g_ref, w_ref, q1_ref, k1_ref, v1_ref, q4_ref, k4_ref, v4_ref,
                   q16_ref, k16_ref, v16_ref, qb_ref, kb_ref, vb_ref, nat_ref, cls_ref):
    tm = x_ref.shape[0]
    h = _rmsnorm(x_ref[...], g_ref[...]).astype(_BF16)
    scale = HEAD_DIM ** -0.5 * LOG2E

    def project(col, width, mul):
        p = jnp.dot(h, w_ref[:, col:col + width], preferred_element_type=_F32)
        return p if mul is None else p * mul

    dilated = ((q1_ref, q4_ref, q16_ref, scale), (k1_ref, k4_ref, k16_ref, None),
               (v1_ref, v4_ref, v16_ref, None))
    for n, (o1_ref, o4_ref, o16_ref, mul) in enumerate(dilated):
        p = project(n * D_MIX, D_MIX, mul)
        o1_ref[...] = p.astype(_BF16)
        for j in range(D_MIX // LANES):
            cols = slice(j * LANES, (j + 1) * LANES)
            nat_ref[n, j] = p[:, cols]
            for c in range(4):
                rows = nat_ref[n, j, pl.ds(c, tm // 4, stride=4), :]
                o4_ref[c, :, cols] = rows.astype(_BF16)
                cls_ref[n, j, c] = rows
            for c in range(4):
                for b in range(4):
                    rows = cls_ref[n, j, c, pl.ds(b, tm // 16, stride=4), :]
                    o16_ref[4 * b + c, :, cols] = rows.astype(_BF16)
    qb = project(3 * D_MIX, D_MIX, scale)
    qb_ref[...] = _regroup_heads(qb, to_kernel_order=True).astype(_BF16)
    kb_ref[...] = project(4 * D_MIX, D_KV_WIN, None).astype(_BF16)
    vb_ref[...] = project(4 * D_MIX + D_KV_WIN, D_KV_WIN, None).astype(_BF16)


def _class_block(tm, dil, width, tiles_per_batch):
    return pl.BlockSpec((None, dil, tm // dil, width),
                        lambda i: (i // tiles_per_batch, 0, i % tiles_per_batch, 0))


def _layer_block(stacked, layer):
    return pl.BlockSpec((None,) + stacked.shape[1:], lambda i: (layer, 0, 0),
                        pipeline_mode=pl.Buffered(1))


def _inproj(x2, g, w, layer, batch, seq):
    n = x2.shape[0]
    tm = TOK_TILE_PROJ
    tpb = seq // tm
    row = lambda width: pl.BlockSpec((tm, width), lambda i: (i, 0))
    flat = lambda width: jax.ShapeDtypeStruct((n, width), _BF16)
    cls = lambda dil: jax.ShapeDtypeStruct((batch, dil, seq // dil, D_MIX), _BF16)
    return pl.pallas_call(
        _inproj_kernel,
        grid=(n // tm,),
        in_specs=[row(D_MODEL),
                  _layer_block(g, layer), _layer_block(w, layer)],
        out_specs=([row(D_MIX)] * 3 + [_class_block(tm, 4, D_MIX, tpb)] * 3
                   + [_class_block(tm, 16, D_MIX, tpb)] * 3
                   + [row(D_MIX), row(D_KV_WIN), row(D_KV_WIN)]),
        out_shape=([flat(D_MIX)] * 3 + [cls(4)] * 3 + [cls(16)] * 3
                   + [flat(D_MIX), flat(D_KV_WIN), flat(D_KV_WIN)]),
        scratch_shapes=[pltpu.VMEM((3, D_MIX // LANES, tm, LANES), _F32),
                        pltpu.VMEM((3, D_MIX // LANES, 4, tm // 4, LANES), _F32)],
        compiler_params=pltpu.CompilerParams(dimension_semantics=("arbitrary",),
                                             vmem_limit_bytes=VMEM_LIMIT),
        name="inproj",
    )(x2, g, w)


def _block_window(t, tq, bi, side, length):
    kw = Q_BLOCK + 2 * side
    q0 = t * tq + bi * Q_BLOCK
    kstart = jnp.clip(q0 - side, 0, length - kw)
    var = lax.shift_right_logical(q0 - kstart, side.bit_length() - 1)
    return pl.multiple_of(bi * Q_BLOCK, Q_BLOCK), pl.multiple_of(kstart, side), var


def _split_heads(qp, lo):
    zero = jnp.zeros_like(qp)
    return jnp.concatenate([jnp.where(lo, qp, zero), jnp.where(lo, zero, qp)], axis=0)


def _with_ones(vp):
    return jnp.concatenate([vp, jnp.ones_like(vp)], axis=1)


def _banded_attn_kernel(*refs, side, length, tq, shared_kv, with_sink, layer):
    if with_sink:
        sink_ref, q_ref, k_ref, v_ref, bias_ref, o_ref = refs
    else:
        q_ref, k_ref, v_ref, bias_ref, o_ref, st_ref = refs
    t = pl.program_id(2)
    kw = Q_BLOCK + 2 * side
    n_pairs = N_HEADS // 2
    blocks_per_class = tq // Q_BLOCK
    n_iter = q_ref.shape[0] * blocks_per_class
    lane = lax.broadcasted_iota(jnp.int32, (Q_BLOCK, LANES), 1)
    lo = lane < HEAD_DIM
    pair_cols = [slice(p * LANES, (p + 1) * LANES) for p in range(n_pairs)]
    kv_cols = [slice(0, LANES)] * n_pairs if shared_kv else pair_cols
    nt_dims = (((1,), (1,)), ((), ()))

    pair_groups = [list(range(n_pairs))] if shared_kv else [[n] for n in range(n_pairs)]

    def sink_rows(pairs):
        return jnp.concatenate([jnp.full((Q_BLOCK, LANES), sink_ref[layer, WIN_HEAD_ORDER[h]], _F32)
                                for h in range(2 * pairs[0], 2 * pairs[-1] + 2)], axis=0)

    def group_block(ci, r0, kstart, var, pairs, st):
        rows = slice(2 * pairs[0] * Q_BLOCK, 2 * (pairs[-1] + 1) * Q_BLOCK)
        n_rows = rows.stop - rows.start
        q = jnp.concatenate([_split_heads(q_ref[ci, pl.ds(r0, Q_BLOCK), pair_cols[n]], lo)
                             for n in pairs], axis=0)
        cols = kv_cols[pairs[0]]
        s = lax.dot_general(q, k_ref[ci, pl.ds(kstart, kw), cols], nt_dims,
                            preferred_element_type=_F32)
        s = s + bias_ref[var, rows, :]
        m = jnp.broadcast_to(jnp.max(s, axis=-1, keepdims=True), (n_rows, LANES))
        if with_sink:
            m = jnp.maximum(m, sink_rows(pairs))
        p = jnp.concatenate([jnp.exp2(s[:, c:c + LANES] - m) for c in range(0, kw, LANES)],
                            axis=1).astype(_BF16)
        ol = jnp.dot(p, _with_ones(v_ref[ci, pl.ds(kstart, kw), cols]), preferred_element_type=_F32)
        o = ol[:, :LANES]
        l = ol[:, LANES:]
        den = l + jnp.exp2(sink_rows(pairs) - m) if with_sink else l
        for k, n in enumerate(pairs):
            top, mid, end = 2 * k * Q_BLOCK, (2 * k + 1) * Q_BLOCK, (2 * k + 2) * Q_BLOCK
            o_pair = jnp.where(lo, o[top:mid], o[mid:end])
            den_pair = jnp.where(lo, den[top:mid], den[mid:end])
            o_ref[ci, pl.ds(r0, Q_BLOCK), pair_cols[n]] = (o_pair * (1.0 / den_pair)).astype(_BF16)
            if not with_sink:
                for h, hrows in ((2 * n, slice(top, mid)), (2 * n + 1, slice(mid, end))):
                    st = jnp.where(lane == h, m[hrows], st)
                    st = jnp.where(lane == N_HEADS + h, l[hrows], st)
        return st

    def body(i, carry):
        ci = i // blocks_per_class
        r0, kstart, var = _block_window(t, tq, i % blocks_per_class, side, length)
        st = jnp.zeros((Q_BLOCK, LANES), _F32)
        for pairs in pair_groups:
            st = group_block(ci, r0, kstart, var, pairs, st)
        if not with_sink:
            st_ref[ci, pl.ds(r0, Q_BLOCK), :] = st
        return carry

    lax.fori_loop(0, n_iter, body, 0, unroll=UNROLL_BLOCKS)


def _banded_attention(q, k, v, bias, side, sink=None, layer=0):
    batch, classes, length, _ = q.shape
    tq = min(Q_TILE, length)
    group = min(classes, Q_TILE // tq)
    kw = Q_BLOCK + 2 * side
    with_sink = sink is not None
    tile = lambda width: pl.BlockSpec((None, group, tq, width), lambda b, r, t: (b, r, t, 0))
    kv_full = pl.BlockSpec((None, group, length, k.shape[-1]), lambda b, r, t: (b, r, 0, 0))
    in_specs = [tile(D_MIX), kv_full, kv_full, pl.BlockSpec(bias.shape, lambda b, r, t: (0, 0, 0))]
    out_specs = [tile(D_MIX)]
    out_shape = [jax.ShapeDtypeStruct(q.shape, _BF16)]
    args = [q, k, v, bias]
    if with_sink:
        in_specs.insert(0, pl.BlockSpec(memory_space=pltpu.SMEM))
        args.insert(0, sink)
    else:
        out_specs.append(tile(LANES))
        out_shape.append(jax.ShapeDtypeStruct((batch, classes, length, LANES), _F32))
    return pl.pallas_call(
        functools.partial(_banded_attn_kernel, side=side, length=length, tq=tq,
                          shared_kv=k.shape[-1] == LANES, with_sink=with_sink, layer=layer),
        grid=(batch, classes // group, length // tq),
        in_specs=in_specs,
        out_specs=out_specs,
        out_shape=out_shape,
        compiler_params=pltpu.CompilerParams(dimension_semantics=("arbitrary",) * 3,
                                             vmem_limit_bytes=VMEM_LIMIT),
        name="win_attn" if with_sink else f"dil_attn_c{classes}",
    )(*args)


def _to_token_order(src4_ref, src16_ref, nat4_ref, nat16_ref, cls_ref):
    n_tiles, tm, _ = nat4_ref.shape
    for j in range(n_tiles):
        cols = slice(j * LANES, (j + 1) * LANES)
        for c in range(4):
            nat4_ref[j, pl.ds(c, tm // 4, stride=4), :] = src4_ref[c, :, cols].astype(_F32)
            for b in range(4):
                cls_ref[j, c, pl.ds(b, tm // 16, stride=4), :] = (
                    src16_ref[4 * b + c, :, cols].astype(_F32))
        for c in range(4):
            nat16_ref[j, pl.ds(c, tm // 4, stride=4), :] = cls_ref[j, c]


def _lane_tiles(ref):
    return jnp.concatenate([ref[j] for j in range(ref.shape[0])], axis=-1)


def _mix_ffn_kernel(x_ref, o1_ref, o4_ref, o16_ref, s1_ref, s4_ref, s16_ref, yb_ref, e_ref,
                    gd_ref, gw_ref, wout_ref, gf_ref, wg_ref, wu_ref, wd_ref, gfin_ref,
                    out_ref, act_ref, on4_ref, on16_ref, ocls_ref, sn4_ref, sn16_ref, scls_ref,
                    *, final):
    tm = x_ref.shape[0]
    _to_token_order(o4_ref, o16_ref, on4_ref, on16_ref, ocls_ref)
    _to_token_order(s4_ref, s16_ref, sn4_ref, sn16_ref, scls_ref)
    lane = lax.broadcasted_iota(jnp.int32, (tm, LANES), 1)
    stats = [s1_ref[...], sn4_ref[0], sn16_ref[0]]
    outs = [o1_ref[...].astype(_F32), _lane_tiles(on4_ref), _lane_tiles(on16_ref)]
    big = jnp.maximum(jnp.maximum(stats[0], stats[1]), stats[2])
    weights = [jnp.exp2(st - big) * pltpu.roll(st, LANES - N_HEADS, axis=1) for st in stats]
    inv = 1.0 / (weights[0] + weights[1] + weights[2])
    ya = jnp.zeros((tm, D_MIX), _F32)
    for w, o in zip(weights, outs):
        c = jnp.where(lane < N_HEADS, w * inv, 0.0)
        c_hi = c.astype(_BF16)
        c_lo = (c - c_hi.astype(_F32)).astype(_BF16)
        spread = jnp.dot(jnp.concatenate([c_hi, c_lo], axis=1), e_ref[...],
                         preferred_element_type=_F32)
        ya = ya + spread * o
    ya = _rmsnorm(ya, gd_ref[...]).astype(_BF16)
    yb = _regroup_heads(yb_ref[...].astype(_F32), to_kernel_order=False)
    yb = _rmsnorm(yb, gw_ref[...]).astype(_BF16)
    x1 = (x_ref[...]
          + jnp.dot(ya, wout_ref[:D_MIX, :], preferred_element_type=_F32)
          + jnp.dot(yb, wout_ref[D_MIX:, :], preferred_element_type=_F32))
    h = _rmsnorm(x1, gf_ref[...]).astype(_BF16)
    for c0 in range(0, D_FF, FF_CHUNK):
        g = jnp.dot(h, wg_ref[:, c0:c0 + FF_CHUNK], preferred_element_type=_F32)
        u = jnp.dot(h, wu_ref[:, c0:c0 + FF_CHUNK], preferred_element_type=_F32)
        act_ref[:, c0:c0 + FF_CHUNK] = (g * jax.nn.sigmoid(g) * u).astype(_BF16)
    out = x1 + jnp.dot(act_ref[...], wd_ref[...], preferred_element_type=_F32)
    if final:
        out = _rmsnorm(out, gfin_ref[...])
    out_ref[...] = out


def _mix_ffn(x2, outs, stats, yb, spread, gd, gw, wout, gf, wg, wu, wd, gfin, layer, seq, final):
    n = x2.shape[0]
    tm = TOK_TILE_FFN
    tpb = seq // tm
    row = lambda width: pl.BlockSpec((tm, width), lambda i: (i, 0))
    whole = lambda a: pl.BlockSpec(a.shape, lambda i: (0, 0), pipeline_mode=pl.Buffered(1))
    per_pattern = lambda width: [row(width), _class_block(tm, 4, width, tpb),
                                 _class_block(tm, 16, width, tpb)]
    nat = lambda width: pltpu.VMEM((width // LANES, tm, LANES), _F32)
    cls = lambda width: pltpu.VMEM((width // LANES, 4, tm // 4, LANES), _F32)
    return pl.pallas_call(
        functools.partial(_mix_ffn_kernel, final=final),
        grid=(n // tm,),
        in_specs=[row(D_MODEL), *per_pattern(D_MIX), *per_pattern(LANES), row(D_MIX),
                  whole(spread), *[_layer_block(a, layer) for a in (gd, gw, wout, gf, wg, wu, wd)],
                  whole(gfin)],
        out_specs=row(D_MODEL),
        out_shape=jax.ShapeDtypeStruct((n, D_MODEL), _F32),
        scratch_shapes=[pltpu.VMEM((tm, D_FF), _BF16), nat(D_MIX), nat(D_MIX), cls(D_MIX),
                        nat(LANES), nat(LANES), cls(LANES)],
        compiler_params=pltpu.CompilerParams(dimension_semantics=("arbitrary",),
                                             vmem_limit_bytes=VMEM_LIMIT),
        name="mix_ffn",
    )(x2, *outs, *stats, yb, spread, gd, gw, wout, gf, wg, wu, wd, gfin)


def kernel(x, g_mix, w_in, g_out_dil, g_out_win, sink, w_out, g_ffn, w_gate, w_up, w_down, g_final):
    batch, seq, _ = x.shape
    depth = w_in.shape[0]
    slopes = _alibi_slopes(N_ALIBI)
    slopes_win = slopes[:N_HEADS][np.array(WIN_HEAD_ORDER)]
    slopes_dil = slopes[N_HEADS:]
    bias_dil = [jnp.asarray(_bias_table(w // (2 * d), d, slopes_dil)) for w, d in DIL_PATTERNS]
    bias_win = jnp.asarray(_bias_table(WIN_SIDE, 1, slopes_win))
    spread = np.zeros((2, LANES, D_MIX), np.float32)
    for h in range(N_HEADS):
        spread[:, h, h * HEAD_DIM:(h + 1) * HEAD_DIM] = 1.0
    spread = jnp.asarray(spread.reshape(2 * LANES, D_MIX), _BF16)

    w_in_k, w_out_k, w_gate_k, w_up_k, w_down_k = (
        w.astype(_BF16) for w in (w_in, w_out, w_gate, w_up, w_down))
    sink_k = sink * LOG2E
    vecs = lambda g: g.reshape(depth, 1, -1)

    x2 = x.reshape(batch * seq, D_MODEL)
    for i in range(depth):
        q1, k1, v1, q4, k4, v4, q16, k16, v16, qb, kb, vb = _inproj(
            x2, vecs(g_mix), w_in_k, i, batch, seq)
        as_class = lambda a: a.reshape(batch, 1, seq, a.shape[-1])
        qkv = ((as_class(q1), as_class(k1), as_class(v1)), (q4, k4, v4), (q16, k16, v16))
        outs, stats = [], []
        for (window, dil), (q, k, v), bias in zip(DIL_PATTERNS, qkv, bias_dil):
            o, st = _banded_attention(q, k, v, bias, window // (2 * dil))
            outs.append(o)
            stats.append(st)
        outs[0] = outs[0].reshape(batch * seq, D_MIX)
        stats[0] = stats[0].reshape(batch * seq, LANES)
        (yb,) = _banded_attention(as_class(qb), as_class(kb), as_class(vb), bias_win, WIN_SIDE,
                                  sink_k, i)
        x2 = _mix_ffn(x2, outs, stats, yb.reshape(batch * seq, D_MIX), spread, vecs(g_out_dil),
                      vecs(g_out_win), w_out_k, vecs(g_ffn), w_gate_k, w_up_k, w_down_k,
                      g_final.reshape(1, -1), i, seq, final=(i == depth - 1))
    return x2.reshape(batch, seq, D_MODEL)
```

```python
import functools
import math

import jax
import jax.numpy as jnp
import numpy as np
from jax import lax
from jax.experimental import pallas as pl
from jax.experimental.pallas import tpu as pltpu

D_MODEL = 1024
HEAD_DIM = 64
N_HEADS = 8
N_KV_WIN = 2
DIL_PATTERNS = ((128, 1), (512, 4), (2048, 16))
WIN_SIDE = 128
D_MIX = N_HEADS * HEAD_DIM
D_KV_WIN = N_KV_WIN * HEAD_DIM
D_FF = 2816
EPS = 1e-6
NEG = -1e30
N_ALIBI = 2 * N_HEADS
LOG2E = math.log2(math.e)

LANES = 128
FF_CHUNK = 256
Q_BLOCK = 128
Q_TILE = 2048
UNROLL_BLOCKS = 8
SCORE_LOOKAHEAD = 1
TOK_TILE_PROJ = 1024
TOK_TILE_FFN = 512
VMEM_LIMIT = 56 * 1024 * 1024

WIN_HEAD_ORDER = (0, 4, 1, 5, 2, 6, 3, 7)

_F32 = jnp.float32
_BF16 = jnp.bfloat16


def _alibi_slopes(n):
    return np.array([2.0 ** (-8.0 * (i + 1) / n) for i in range(n)], dtype=np.float32)


def _bias_table(side, dil, slopes):
    kw = Q_BLOCK + 2 * side
    i = np.arange(Q_BLOCK)[:, None]
    j = np.arange(kw)[None, :]
    out = np.empty((3, len(slopes), Q_BLOCK, kw), np.float32)
    for v in range(3):
        rel = np.abs(j - i - side * v)
        dist = rel.astype(np.float32) * np.float32(dil)
        for h, sl in enumerate(slopes):
            out[v, h] = np.where(rel <= side, -(np.float32(sl) * dist) * np.float32(LOG2E),
                                 np.float32(NEG))
    return out.reshape(3, len(slopes) * Q_BLOCK, kw)


def _rmsnorm(x, g):
    return x * lax.rsqrt(jnp.mean(x * x, axis=-1, keepdims=True) + EPS) * g


def _regroup_heads(y, to_kernel_order):
    lo = lax.broadcasted_iota(jnp.int32, (y.shape[0], LANES), 1) < HEAD_DIM
    t = [y[:, j * LANES:(j + 1) * LANES] for j in range(D_MIX // LANES)]
    r = [pltpu.roll(x, HEAD_DIM, axis=1) for x in t]
    if to_kernel_order:
        out = [jnp.where(lo, t[0], r[2]), jnp.where(lo, r[0], t[2]),
               jnp.where(lo, t[1], r[3]), jnp.where(lo, r[1], t[3])]
    else:
        out = [jnp.where(lo, t[0], r[1]), jnp.where(lo, t[2], r[3]),
               jnp.where(lo, r[0], t[1]), jnp.where(lo, r[2], t[3])]
    return jnp.concatenate(out, axis=1)


def _inproj_kernel(x_ref, g_ref, w_ref, q1_ref, k1_ref, v1_ref, q4_ref, k4_ref, v4_ref,
                   q16_ref, k16_ref, v16_ref, qb_ref, kb_ref, vb_ref, nat_ref, cls_ref):
    tm = x_ref.shape[0]
    h = _rmsnorm(x_ref[...], g_ref[...]).astype(_BF16)
    scale = HEAD_DIM ** -0.5 * LOG2E

    def project(col, width, mul):
        p = jnp.dot(h, w_ref[:, col:col + width], preferred_element_type=_F32)
        return p if mul is None else p * mul

    dilated = ((q1_ref, q4_ref, q16_ref, scale), (k1_ref, k4_ref, k16_ref, None),
               (v1_ref, v4_ref, v16_ref, None))
    for n, (o1_ref, o4_ref, o16_ref, mul) in enumerate(dilated):
        p = project(n * D_MIX, D_MIX, mul)
        o1_ref[...] = p.astype(_BF16)
        for j in range(D_MIX // LANES):
            cols = slice(j * LANES, (j + 1) * LANES)
            nat_ref[n, j] = p[:, cols]
            for c in range(4):
                rows = nat_ref[n, j, pl.ds(c, tm // 4, stride=4), :]
                o4_ref[c, :, cols] = rows.astype(_BF16)
                cls_ref[n, j, c] = rows
            for c in range(4):
                for b in range(4):
                    rows = cls_ref[n, j, c, pl.ds(b, tm // 16, stride=4), :]
                    o16_ref[4 * b + c, :, cols] = rows.astype(_BF16)
    qb = project(3 * D_MIX, D_MIX, scale)
    qb_ref[...] = _regroup_heads(qb, to_kernel_order=True).astype(_BF16)
    kb_ref[...] = project(4 * D_MIX, D_KV_WIN, None).astype(_BF16)
    vb_ref[...] = project(4 * D_MIX + D_KV_WIN, D_KV_WIN, None).astype(_BF16)


def _class_block(tm, dil, width, tiles_per_batch):
    return pl.BlockSpec((None, dil, tm // dil, width),
                        lambda i: (i // tiles_per_batch, 0, i % tiles_per_batch, 0))


def _layer_block(stacked, layer):
    return pl.BlockSpec((None,) + stacked.shape[1:], lambda i: (layer, 0, 0),
                        pipeline_mode=pl.Buffered(1))


def _inproj(x2, g, w, layer, batch, seq):
    n = x2.shape[0]
    tm = TOK_TILE_PROJ
    tpb = seq // tm
    row = lambda width: pl.BlockSpec((tm, width), lambda i: (i, 0))
    flat = lambda width: jax.ShapeDtypeStruct((n, width), _BF16)
    cls = lambda dil: jax.ShapeDtypeStruct((batch, dil, seq // dil, D_MIX), _BF16)
    return pl.pallas_call(
        _inproj_kernel,
        grid=(n // tm,),
        in_specs=[row(D_MODEL),
                  _layer_block(g, layer), _layer_block(w, layer)],
        out_specs=([row(D_MIX)] * 3 + [_class_block(tm, 4, D_MIX, tpb)] * 3
                   + [_class_block(tm, 16, D_MIX, tpb)] * 3
                   + [row(D_MIX), row(D_KV_WIN), row(D_KV_WIN)]),
        out_shape=([flat(D_MIX)] * 3 + [cls(4)] * 3 + [cls(16)] * 3
                   + [flat(D_MIX), flat(D_KV_WIN), flat(D_KV_WIN)]),
        scratch_shapes=[pltpu.VMEM((3, D_MIX // LANES, tm, LANES), _F32),
                        pltpu.VMEM((3, D_MIX // LANES, 4, tm // 4, LANES), _F32)],
        compiler_params=pltpu.CompilerParams(dimension_semantics=("arbitrary",),
                                             vmem_limit_bytes=VMEM_LIMIT),
        name="inproj",
    )(x2, g, w)


def _block_window(t, tq, bi, side, length):
    kw = Q_BLOCK + 2 * side
    q0 = t * tq + bi * Q_BLOCK
    kstart = jnp.clip(q0 - side, 0, length - kw)
    var = lax.shift_right_logical(q0 - kstart, side.bit_length() - 1)
    return pl.multiple_of(bi * Q_BLOCK, Q_BLOCK), pl.multiple_of(kstart, side), var


def _split_heads(qp, lo):
    zero = jnp.zeros_like(qp)
    return jnp.concatenate([jnp.where(lo, qp, zero), jnp.where(lo, zero, qp)], axis=0)


def _with_ones(vp):
    return jnp.concatenate([vp, jnp.ones_like(vp)], axis=1)


def _banded_attn_kernel(*refs, side, length, tq, shared_kv, with_sink, layer):
    if with_sink:
        sink_ref, q_ref, k_ref, v_ref, bias_ref, o_ref = refs
    else:
        q_ref, k_ref, v_ref, bias_ref, o_ref, st_ref = refs
    t = pl.program_id(2)
    kw = Q_BLOCK + 2 * side
    n_pairs = N_HEADS // 2
    blocks_per_class = tq // Q_BLOCK
    n_iter = q_ref.shape[0] * blocks_per_class
    assert n_iter % UNROLL_BLOCKS == 0, (q_ref.shape, UNROLL_BLOCKS)
    lane = lax.broadcasted_iota(jnp.int32, (Q_BLOCK, LANES), 1)
    lo = lane < HEAD_DIM
    pair_cols = [slice(p * LANES, (p + 1) * LANES) for p in range(n_pairs)]
    kv_cols = [slice(0, LANES)] * n_pairs if shared_kv else pair_cols
    nt_dims = (((1,), (1,)), ((), ()))

    pair_groups = [[0, 1], [2, 3]] if shared_kv else [[n] for n in range(n_pairs)]

    def sink_rows(pairs):
        return jnp.concatenate([jnp.full((Q_BLOCK, LANES), sink_ref[layer, WIN_HEAD_ORDER[h]], _F32)
                                for h in range(2 * pairs[0], 2 * pairs[-1] + 2)], axis=0)

    def group_scores(ci, r0, kstart, var, pairs):
        rows = slice(2 * pairs[0] * Q_BLOCK, 2 * (pairs[-1] + 1) * Q_BLOCK)
        q = jnp.concatenate([_split_heads(q_ref[ci, pl.ds(r0, Q_BLOCK), pair_cols[n]], lo)
                             for n in pairs], axis=0)
        s = lax.dot_general(q, k_ref[ci, pl.ds(kstart, kw), kv_cols[pairs[0]]], nt_dims,
                            preferred_element_type=_F32)
        return s + bias_ref[var, rows, :]

    def group_outputs(ci, r0, kstart, pairs, s, st):
        n_rows = s.shape[0]
        cols = kv_cols[pairs[0]]
        m = jnp.broadcast_to(jnp.max(s, axis=-1, keepdims=True), (n_rows, LANES))
        if with_sink:
            m = jnp.maximum(m, sink_rows(pairs))
        p = jnp.concatenate([jnp.exp2(s[:, c:c + LANES] - m) for c in range(0, kw, LANES)],
                            axis=1).astype(_BF16)
        ol = jnp.dot(p, _with_ones(v_ref[ci, pl.ds(kstart, kw), cols]), preferred_element_type=_F32)
        o = ol[:, :LANES]
        l = ol[:, LANES:]
        den = l + jnp.exp2(sink_rows(pairs) - m) if with_sink else l
        for k, n in enumerate(pairs):
            top, mid, end = 2 * k * Q_BLOCK, (2 * k + 1) * Q_BLOCK, (2 * k + 2) * Q_BLOCK
            o_pair = jnp.where(lo, o[top:mid], o[mid:end])
            den_pair = jnp.where(lo, den[top:mid], den[mid:end])
            o_ref[ci, pl.ds(r0, Q_BLOCK), pair_cols[n]] = (o_pair * (1.0 / den_pair)).astype(_BF16)
            if not with_sink:
                for h, hrows in ((2 * n, slice(top, mid)), (2 * n + 1, slice(mid, end))):
                    st = jnp.where(lane == h, m[hrows], st)
                    st = jnp.where(lane == N_HEADS + h, l[hrows], st)
        return st

    def body(g, carry):
        units = [(j, pairs) for j in range(UNROLL_BLOCKS) for pairs in pair_groups]
        where, stats, pending = {}, {}, []
        for u in range(len(units) + SCORE_LOOKAHEAD):
            if u < len(units):
                j, pairs = units[u]
                if j not in where:
                    i = g * UNROLL_BLOCKS + j
                    where[j] = (i // blocks_per_class,) + _block_window(
                        t, tq, i % blocks_per_class, side, length)
                    stats[j] = jnp.zeros((Q_BLOCK, LANES), _F32)
                ci, r0, kstart, var = where[j]
                pending.append((j, pairs, group_scores(ci, r0, kstart, var, pairs)))
            if u >= SCORE_LOOKAHEAD:
                j, pairs, s = pending.pop(0)
                ci, r0, kstart, _ = where[j]
                stats[j] = group_outputs(ci, r0, kstart, pairs, s, stats[j])
                if not with_sink and pairs is pair_groups[-1]:
                    st_ref[ci, pl.ds(r0, Q_BLOCK), :] = stats[j]
        return carry

    lax.fori_loop(0, n_iter // UNROLL_BLOCKS, body, 0)


def _banded_attention(q, k, v, bias, side, sink=None, layer=0):
    batch, classes, length, _ = q.shape
    tq = min(Q_TILE, length)
    group = min(classes, Q_TILE // tq)
    kw = Q_BLOCK + 2 * side
    with_sink = sink is not None
    tile = lambda width: pl.BlockSpec((None, group, tq, width), lambda b, r, t: (b, r, t, 0))
    kv_full = pl.BlockSpec((None, group, length, k.shape[-1]), lambda b, r, t: (b, r, 0, 0))
    in_specs = [tile(D_MIX), kv_full, kv_full, pl.BlockSpec(bias.shape, lambda b, r, t: (0, 0, 0))]
    out_specs = [tile(D_MIX)]
    out_shape = [jax.ShapeDtypeStruct(q.shape, _BF16)]
    args = [q, k, v, bias]
    if with_sink:
        in_specs.insert(0, pl.BlockSpec(memory_space=pltpu.SMEM))
        args.insert(0, sink)
    else:
        out_specs.append(tile(LANES))
        out_shape.append(jax.ShapeDtypeStruct((batch, classes, length, LANES), _F32))
    return pl.pallas_call(
        functools.partial(_banded_attn_kernel, side=side, length=length, tq=tq,
                          shared_kv=k.shape[-1] == LANES, with_sink=with_sink, layer=layer),
        grid=(batch, classes // group, length // tq),
        in_specs=in_specs,
        out_specs=out_specs,
        out_shape=out_shape,
        compiler_params=pltpu.CompilerParams(dimension_semantics=("arbitrary",) * 3,
                                             vmem_limit_bytes=VMEM_LIMIT),
        name="win_attn" if with_sink else f"dil_attn_c{classes}",
    )(*args)


def _to_token_order(src4_ref, src16_ref, nat4_ref, nat16_ref, cls_ref):
    n_tiles, tm, _ = nat4_ref.shape
    for j in range(n_tiles):
        cols = slice(j * LANES, (j + 1) * LANES)
        for c in range(4):
            nat4_ref[j, pl.ds(c, tm // 4, stride=4), :] = src4_ref[c, :, cols].astype(_F32)
            for b in range(4):
                cls_ref[j, c, pl.ds(b, tm // 16, stride=4), :] = (
                    src16_ref[4 * b + c, :, cols].astype(_F32))
        for c in range(4):
            nat16_ref[j, pl.ds(c, tm // 4, stride=4), :] = cls_ref[j, c]


def _lane_tiles(ref):
    return jnp.concatenate([ref[j] for j in range(ref.shape[0])], axis=-1)


def _mix_ffn_kernel(x_ref, o1_ref, o4_ref, o16_ref, s1_ref, s4_ref, s16_ref, yb_ref, e_ref,
                    gd_ref, gw_ref, wout_ref, gf_ref, wg_ref, wu_ref, wd_ref, gfin_ref,
                    out_ref, act_ref, on4_ref, on16_ref, ocls_ref, sn4_ref, sn16_ref, scls_ref,
                    *, final):
    tm = x_ref.shape[0]
    _to_token_order(o4_ref, o16_ref, on4_ref, on16_ref, ocls_ref)
    _to_token_order(s4_ref, s16_ref, sn4_ref, sn16_ref, scls_ref)
    lane = lax.broadcasted_iota(jnp.int32, (tm, LANES), 1)
    stats = [s1_ref[...], sn4_ref[0], sn16_ref[0]]
    outs = [o1_ref[...].astype(_F32), _lane_tiles(on4_ref), _lane_tiles(on16_ref)]
    big = jnp.maximum(jnp.maximum(stats[0], stats[1]), stats[2])
    weights = [jnp.exp2(st - big) * pltpu.roll(st, LANES - N_HEADS, axis=1) for st in stats]
    inv = 1.0 / (weights[0] + weights[1] + weights[2])
    ya = jnp.zeros((tm, D_MIX), _F32)
    for w, o in zip(weights, outs):
        c = jnp.where(lane < N_HEADS, w * inv, 0.0)
        c_hi = c.astype(_BF16)
        c_lo = (c - c_hi.astype(_F32)).astype(_BF16)
        spread = jnp.dot(jnp.concatenate([c_hi, c_lo], axis=1), e_ref[...],
                         preferred_element_type=_F32)
        ya = ya + spread * o
    ya = _rmsnorm(ya, gd_ref[...]).astype(_BF16)
    yb = _regroup_heads(yb_ref[...].astype(_F32), to_kernel_order=False)
    yb = _rmsnorm(yb, gw_ref[...]).astype(_BF16)
    x1 = (x_ref[...]
          + jnp.dot(ya, wout_ref[:D_MIX, :], preferred_element_type=_F32)
          + jnp.dot(yb, wout_ref[D_MIX:, :], preferred_element_type=_F32))
    h = _rmsnorm(x1, gf_ref[...]).astype(_BF16)
    for c0 in range(0, D_FF, FF_CHUNK):
        g = jnp.dot(h, wg_ref[:, c0:c0 + FF_CHUNK], preferred_element_type=_F32)
        u = jnp.dot(h, wu_ref[:, c0:c0 + FF_CHUNK], preferred_element_type=_F32)
        act_ref[:, c0:c0 + FF_CHUNK] = (g * jax.nn.sigmoid(g) * u).astype(_BF16)
    out = x1 + jnp.dot(act_ref[...], wd_ref[...], preferred_element_type=_F32)
    if final:
        out = _rmsnorm(out, gfin_ref[...])
    out_ref[...] = out


def _mix_ffn(x2, outs, stats, yb, spread, gd, gw, wout, gf, wg, wu, wd, gfin, layer, seq, final):
    n = x2.shape[0]
    tm = TOK_TILE_FFN
    tpb = seq // tm
    row = lambda width: pl.BlockSpec((tm, width), lambda i: (i, 0))
    whole = lambda a: pl.BlockSpec(a.shape, lambda i: (0, 0), pipeline_mode=pl.Buffered(1))
    per_pattern = lambda width: [row(width), _class_block(tm, 4, width, tpb),
                                 _class_block(tm, 16, width, tpb)]
    nat = lambda width: pltpu.VMEM((width // LANES, tm, LANES), _F32)
    cls = lambda width: pltpu.VMEM((width // LANES, 4, tm // 4, LANES), _F32)
    return pl.pallas_call(
        functools.partial(_mix_ffn_kernel, final=final),
        grid=(n // tm,),
        in_specs=[row(D_MODEL), *per_pattern(D_MIX), *per_pattern(LANES), row(D_MIX),
                  whole(spread), *[_layer_block(a, layer) for a in (gd, gw, wout, gf, wg, wu, wd)],
                  whole(gfin)],
        out_specs=row(D_MODEL),
        out_shape=jax.ShapeDtypeStruct((n, D_MODEL), _F32),
        scratch_shapes=[pltpu.VMEM((tm, D_FF), _BF16), nat(D_MIX), nat(D_MIX), cls(D_MIX),
                        nat(LANES), nat(LANES), cls(LANES)],
        compiler_params=pltpu.CompilerParams(dimension_semantics=("arbitrary",),
                                             vmem_limit_bytes=VMEM_LIMIT),
        name="mix_ffn",
    )(x2, *outs, *stats, yb, spread, gd, gw, wout, gf, wg, wu, wd, gfin)


def kernel(x, g_mix, w_in, g_out_dil, g_out_win, sink, w_out, g_ffn, w_gate, w_up, w_down, g_final):
    batch, seq, _ = x.shape
    depth = w_in.shape[0]
    slopes = _alibi_slopes(N_ALIBI)
    slopes_win = slopes[:N_HEADS][np.array(WIN_HEAD_ORDER)]
    slopes_dil = slopes[N_HEADS:]
    bias_dil = [jnp.asarray(_bias_table(w // (2 * d), d, slopes_dil)) for w, d in DIL_PATTERNS]
    bias_win = jnp.asarray(_bias_table(WIN_SIDE, 1, slopes_win))
    spread = np.zeros((2, LANES, D_MIX), np.float32)
    for h in range(N_HEADS):
        spread[:, h, h * HEAD_DIM:(h + 1) * HEAD_DIM] = 1.0
    spread = jnp.asarray(spread.reshape(2 * LANES, D_MIX), _BF16)

    w_in_k, w_out_k, w_gate_k, w_up_k, w_down_k = (
        w.astype(_BF16) for w in (w_in, w_out, w_gate, w_up, w_down))
    sink_k = sink * LOG2E
    vecs = lambda g: g.reshape(depth, 1, -1)

    x2 = x.reshape(batch * seq, D_MODEL)
    for i in range(depth):
        q1, k1, v1, q4, k4, v4, q16, k16, v16, qb, kb, vb = _inproj(
            x2, vecs(g_mix), w_in_k, i, batch, seq)
        as_class = lambda a: a.reshape(batch, 1, seq, a.shape[-1])
        qkv = ((as_class(q1), as_class(k1), as_class(v1)), (q4, k4, v4), (q16, k16, v16))
        outs, stats = [], []
        for (window, dil), (q, k, v), bias in zip(DIL_PATTERNS, qkv, bias_dil):
            o, st = _banded_attention(q, k, v, bias, window // (2 * dil))
            outs.append(o)
            stats.append(st)
        outs[0] = outs[0].reshape(batch * seq, D_MIX)
        stats[0] = stats[0].reshape(batch * seq, LANES)
        (yb,) = _banded_attention(as_class(qb), as_class(kb), as_class(vb), bias_win, WIN_SIDE,
                                  sink_k, i)
        x2 = _mix_ffn(x2, outs, stats, yb.reshape(batch * seq, D_MIX), spread, vecs(g_out_dil),
                      vecs(g_out_win), w_out_k, vecs(g_ffn), w_gate_k, w_up_k, w_down_k,
                      g_final.reshape(1, -1), i, seq, final=(i == depth - 1))
    return x2.reshape(batch, seq, D_MODEL)
```

```python
import functools
import math

import jax
import jax.numpy as jnp
import numpy as np
from jax import lax
from jax.experimental import pallas as pl
from jax.experimental.pallas import tpu as pltpu

D_MODEL = 1024
HEAD_DIM = 64
N_HEADS = 8
N_KV_WIN = 2
DIL_PATTERNS = ((128, 1), (512, 4), (2048, 16))
WIN_SIDE = 128
D_MIX = N_HEADS * HEAD_DIM
D_KV_WIN = N_KV_WIN * HEAD_DIM
D_FF = 2816
EPS = 1e-6
NEG = -1e30
N_ALIBI = 2 * N_HEADS
LOG2E = math.log2(math.e)

LANES = 128
FF_CHUNK = 256
Q_BLOCK = 128
Q_TILE = 2048
UNROLL_BLOCKS = 16
SCORE_LOOKAHEAD = 1
TOK_TILE_PROJ = 1024
TOK_TILE_FFN = 512
VMEM_LIMIT = 56 * 1024 * 1024

WIN_HEAD_ORDER = (0, 4, 1, 5, 2, 6, 3, 7)

_F32 = jnp.float32
_BF16 = jnp.bfloat16


def _alibi_slopes(n):
    return np.array([2.0 ** (-8.0 * (i + 1) / n) for i in range(n)], dtype=np.float32)


def _bias_table(side, dil, slopes):
    kw = Q_BLOCK + 2 * side
    i = np.arange(Q_BLOCK)[:, None]
    j = np.arange(kw)[None, :]
    out = np.empty((3, len(slopes), Q_BLOCK, kw), np.float32)
    for v in range(3):
        rel = np.abs(j - i - side * v)
        dist = rel.astype(np.float32) * np.float32(dil)
        for h, sl in enumerate(slopes):
            out[v, h] = np.where(rel <= side, -(np.float32(sl) * dist) * np.float32(LOG2E),
                                 np.float32(NEG))
    return out.reshape(3, len(slopes) * Q_BLOCK, kw)


def _rmsnorm(x, g):
    return x * lax.rsqrt(jnp.mean(x * x, axis=-1, keepdims=True) + EPS) * g


def _regroup_heads(y, to_kernel_order):
    lo = lax.broadcasted_iota(jnp.int32, (y.shape[0], LANES), 1) < HEAD_DIM
    t = [y[:, j * LANES:(j + 1) * LANES] for j in range(D_MIX // LANES)]
    r = [pltpu.roll(x, HEAD_DIM, axis=1) for x in t]
    if to_kernel_order:
        out = [jnp.where(lo, t[0], r[2]), jnp.where(lo, r[0], t[2]),
               jnp.where(lo, t[1], r[3]), jnp.where(lo, r[1], t[3])]
    else:
        out = [jnp.where(lo, t[0], r[1]), jnp.where(lo, t[2], r[3]),
               jnp.where(lo, r[0], t[1]), jnp.where(lo, r[2], t[3])]
    return jnp.concatenate(out, axis=1)


def _inproj_kernel(x_ref, g_ref, w_ref, q1_ref, k1_ref, v1_ref, q4_ref, k4_ref, v4_ref,
                   q16_ref, k16_ref, v16_ref, qb_ref, kb_ref, vb_ref, nat_ref, cls_ref):
    tm = x_ref.shape[0]
    h = _rmsnorm(x_ref[...], g_ref[...]).astype(_BF16)
    scale = HEAD_DIM ** -0.5 * LOG2E

    def project(col, width, mul):
        p = jnp.dot(h, w_ref[:, col:col + width], preferred_element_type=_F32)
        return p if mul is None else p * mul

    dilated = ((q1_ref, q4_ref, q16_ref, scale), (k1_ref, k4_ref, k16_ref, None),
               (v1_ref, v4_ref, v16_ref, None))
    for n, (o1_ref, o4_ref, o16_ref, mul) in enumerate(dilated):
        p = project(n * D_MIX, D_MIX, mul)
        o1_ref[...] = p.astype(_BF16)
        for j in range(D_MIX // LANES):
            cols = slice(j * LANES, (j + 1) * LANES)
            nat_ref[n, j] = p[:, cols]
            for c in range(4):
                rows = nat_ref[n, j, pl.ds(c, tm // 4, stride=4), :]
                o4_ref[c, :, cols] = rows.astype(_BF16)
                cls_ref[n, j, c] = rows
            for c in range(4):
                for b in range(4):
                    rows = cls_ref[n, j, c, pl.ds(b, tm // 16, stride=4), :]
                    o16_ref[4 * b + c, :, cols] = rows.astype(_BF16)
    qb = project(3 * D_MIX, D_MIX, scale)
    qb_ref[...] = _regroup_heads(qb, to_kernel_order=True).astype(_BF16)
    kb_ref[...] = project(4 * D_MIX, D_KV_WIN, None).astype(_BF16)
    vb_ref[...] = project(4 * D_MIX + D_KV_WIN, D_KV_WIN, None).astype(_BF16)


def _class_block(tm, dil, width, tiles_per_batch):
    return pl.BlockSpec((None, dil, tm // dil, width),
                        lambda i: (i // tiles_per_batch, 0, i % tiles_per_batch, 0))


def _layer_block(stacked, layer):
    return pl.BlockSpec((None,) + stacked.shape[1:], lambda i: (layer, 0, 0),
                        pipeline_mode=pl.Buffered(1))


def _inproj(x2, g, w, layer, batch, seq):
    n = x2.shape[0]
    tm = TOK_TILE_PROJ
    tpb = seq // tm
    row = lambda width: pl.BlockSpec((tm, width), lambda i: (i, 0))
    flat = lambda width: jax.ShapeDtypeStruct((n, width), _BF16)
    cls = lambda dil: jax.ShapeDtypeStruct((batch, dil, seq // dil, D_MIX), _BF16)
    return pl.pallas_call(
        _inproj_kernel,
        grid=(n // tm,),
        in_specs=[row(D_MODEL),
                  _layer_block(g, layer), _layer_block(w, layer)],
        out_specs=([row(D_MIX)] * 3 + [_class_block(tm, 4, D_MIX, tpb)] * 3
                   + [_class_block(tm, 16, D_MIX, tpb)] * 3
                   + [row(D_MIX), row(D_KV_WIN), row(D_KV_WIN)]),
        out_shape=([flat(D_MIX)] * 3 + [cls(4)] * 3 + [cls(16)] * 3
                   + [flat(D_MIX), flat(D_KV_WIN), flat(D_KV_WIN)]),
        scratch_shapes=[pltpu.VMEM((3, D_MIX // LANES, tm, LANES), _F32),
                        pltpu.VMEM((3, D_MIX // LANES, 4, tm // 4, LANES), _F32)],
        compiler_params=pltpu.CompilerParams(dimension_semantics=("arbitrary",),
                                             vmem_limit_bytes=VMEM_LIMIT),
        name="inproj",
    )(x2, g, w)


def _block_window(t, tq, bi, side, length):
    kw = Q_BLOCK + 2 * side
    q0 = t * tq + bi * Q_BLOCK
    kstart = jnp.clip(q0 - side, 0, length - kw)
    var = lax.shift_right_logical(q0 - kstart, side.bit_length() - 1)
    return pl.multiple_of(bi * Q_BLOCK, Q_BLOCK), pl.multiple_of(kstart, side), var


def _split_heads(qp, lo):
    zero = jnp.zeros_like(qp)
    return jnp.concatenate([jnp.where(lo, qp, zero), jnp.where(lo, zero, qp)], axis=0)


def _with_ones(vp):
    return jnp.concatenate([vp, jnp.ones_like(vp)], axis=1)


def _banded_attn_kernel(*refs, side, length, tq, shared_kv, with_sink, layer):
    if with_sink:
        sink_ref, q_ref, k_ref, v_ref, bias_ref, o_ref = refs
    else:
        q_ref, k_ref, v_ref, bias_ref, o_ref, st_ref = refs
    t = pl.program_id(2)
    kw = Q_BLOCK + 2 * side
    n_pairs = N_HEADS // 2
    blocks_per_class = tq // Q_BLOCK
    n_iter = q_ref.shape[0] * blocks_per_class
    assert n_iter % UNROLL_BLOCKS == 0, (q_ref.shape, UNROLL_BLOCKS)
    lane = lax.broadcasted_iota(jnp.int32, (Q_BLOCK, LANES), 1)
    lo = lane < HEAD_DIM
    pair_cols = [slice(p * LANES, (p + 1) * LANES) for p in range(n_pairs)]
    kv_cols = [slice(0, LANES)] * n_pairs if shared_kv else pair_cols
    nt_dims = (((1,), (1,)), ((), ()))

    pair_groups = [[0, 1], [2, 3]] if shared_kv else [[n] for n in range(n_pairs)]

    def sink_rows(pairs):
        return jnp.concatenate([jnp.full((Q_BLOCK, LANES), sink_ref[layer, WIN_HEAD_ORDER[h]], _F32)
                                for h in range(2 * pairs[0], 2 * pairs[-1] + 2)], axis=0)

    def group_scores(ci, r0, kstart, var, pairs):
        rows = slice(2 * pairs[0] * Q_BLOCK, 2 * (pairs[-1] + 1) * Q_BLOCK)
        q = jnp.concatenate([_split_heads(q_ref[ci, pl.ds(r0, Q_BLOCK), pair_cols[n]], lo)
                             for n in pairs], axis=0)
        s = lax.dot_general(q, k_ref[ci, pl.ds(kstart, kw), kv_cols[pairs[0]]], nt_dims,
                            preferred_element_type=_F32)
        return s + bias_ref[var, rows, :]

    def group_outputs(ci, r0, kstart, pairs, s, st):
        n_rows = s.shape[0]
        cols = kv_cols[pairs[0]]
        m = jnp.broadcast_to(jnp.max(s, axis=-1, keepdims=True), (n_rows, LANES))
        if with_sink:
            m = jnp.maximum(m, sink_rows(pairs))
        p = jnp.concatenate([jnp.exp2(s[:, c:c + LANES] - m) for c in range(0, kw, LANES)],
                            axis=1).astype(_BF16)
        ol = jnp.dot(p, _with_ones(v_ref[ci, pl.ds(kstart, kw), cols]), preferred_element_type=_F32)
        o = ol[:, :LANES]
        l = ol[:, LANES:]
        den = l + jnp.exp2(sink_rows(pairs) - m) if with_sink else l
        for k, n in enumerate(pairs):
            top, mid, end = 2 * k * Q_BLOCK, (2 * k + 1) * Q_BLOCK, (2 * k + 2) * Q_BLOCK
            o_pair = jnp.where(lo, o[top:mid], o[mid:end])
            den_pair = jnp.where(lo, den[top:mid], den[mid:end])
            o_ref[ci, pl.ds(r0, Q_BLOCK), pair_cols[n]] = (o_pair * (1.0 / den_pair)).astype(_BF16)
            if not with_sink:
                for h, hrows in ((2 * n, slice(top, mid)), (2 * n + 1, slice(mid, end))):
                    st = jnp.where(lane == h, m[hrows], st)
                    st = jnp.where(lane == N_HEADS + h, l[hrows], st)
        return st

    def body(g, carry):
        units = [(j, pairs) for j in range(UNROLL_BLOCKS) for pairs in pair_groups]
        where, stats, pending = {}, {}, []
        for u in range(len(units) + SCORE_LOOKAHEAD):
            if u < len(units):
                j, pairs = units[u]
                if j not in where:
                    i = g * UNROLL_BLOCKS + j
                    where[j] = (i // blocks_per_class,) + _block_window(
                        t, tq, i % blocks_per_class, side, length)
                    stats[j] = jnp.zeros((Q_BLOCK, LANES), _F32)
                ci, r0, kstart, var = where[j]
                pending.append((j, pairs, group_scores(ci, r0, kstart, var, pairs)))
            if u >= SCORE_LOOKAHEAD:
                j, pairs, s = pending.pop(0)
                ci, r0, kstart, _ = where[j]
                stats[j] = group_outputs(ci, r0, kstart, pairs, s, stats[j])
                if not with_sink and pairs is pair_groups[-1]:
                    st_ref[ci, pl.ds(r0, Q_BLOCK), :] = stats[j]
        return carry

    lax.fori_loop(0, n_iter // UNROLL_BLOCKS, body, 0)


def _banded_attention(q, k, v, bias, side, sink=None, layer=0):
    batch, classes, length, _ = q.shape
    tq = min(Q_TILE, length)
    group = min(classes, Q_TILE // tq)
    kw = Q_BLOCK + 2 * side
    with_sink = sink is not None
    tile = lambda width: pl.BlockSpec((None, group, tq, width), lambda b, r, t: (b, r, t, 0))
    kv_full = pl.BlockSpec((None, group, length, k.shape[-1]), lambda b, r, t: (b, r, 0, 0))
    in_specs = [tile(D_MIX), kv_full, kv_full, pl.BlockSpec(bias.shape, lambda b, r, t: (0, 0, 0))]
    out_specs = [tile(D_MIX)]
    out_shape = [jax.ShapeDtypeStruct(q.shape, _BF16)]
    args = [q, k, v, bias]
    if with_sink:
        in_specs.insert(0, pl.BlockSpec(memory_space=pltpu.SMEM))
        args.insert(0, sink)
    else:
        out_specs.append(tile(LANES))
        out_shape.append(jax.ShapeDtypeStruct((batch, classes, length, LANES), _F32))
    return pl.pallas_call(
        functools.partial(_banded_attn_kernel, side=side, length=length, tq=tq,
                          shared_kv=k.shape[-1] == LANES, with_sink=with_sink, layer=layer),
        grid=(batch, classes // group, length // tq),
        in_specs=in_specs,
        out_specs=out_specs,
        out_shape=out_shape,
        compiler_params=pltpu.CompilerParams(dimension_semantics=("arbitrary",) * 3,
                                             vmem_limit_bytes=VMEM_LIMIT),
        name="win_attn" if with_sink else f"dil_attn_c{classes}",
    )(*args)


def _to_token_order(src4_ref, src16_ref, nat4_ref, nat16_ref, cls_ref):
    n_tiles, tm, _ = nat4_ref.shape
    for j in range(n_tiles):
        cols = slice(j * LANES, (j + 1) * LANES)
        for c in range(4):
            nat4_ref[j, pl.ds(c, tm // 4, stride=4), :] = src4_ref[c, :, cols].astype(_F32)
            for b in range(4):
                cls_ref[j, c, pl.ds(b, tm // 16, stride=4), :] = (
                    src16_ref[4 * b + c, :, cols].astype(_F32))
        for c in range(4):
            nat16_ref[j, pl.ds(c, tm // 4, stride=4), :] = cls_ref[j, c]


def _lane_tiles(ref):
    return jnp.concatenate([ref[j] for j in range(ref.shape[0])], axis=-1)


def _mix_ffn_kernel(x_ref, o1_ref, o4_ref, o16_ref, s1_ref, s4_ref, s16_ref, yb_ref, e_ref,
                    gd_ref, gw_ref, wout_ref, gf_ref, wg_ref, wu_ref, wd_ref, gfin_ref,
                    out_ref, act_ref, on4_ref, on16_ref, ocls_ref, sn4_ref, sn16_ref, scls_ref,
                    *, final):
    tm = x_ref.shape[0]
    _to_token_order(o4_ref, o16_ref, on4_ref, on16_ref, ocls_ref)
    _to_token_order(s4_ref, s16_ref, sn4_ref, sn16_ref, scls_ref)
    lane = lax.broadcasted_iota(jnp.int32, (tm, LANES), 1)
    stats = [s1_ref[...], sn4_ref[0], sn16_ref[0]]
    outs = [o1_ref[...].astype(_F32), _lane_tiles(on4_ref), _lane_tiles(on16_ref)]
    big = jnp.maximum(jnp.maximum(stats[0], stats[1]), stats[2])
    weights = [jnp.exp2(st - big) * pltpu.roll(st, LANES - N_HEADS, axis=1) for st in stats]
    inv = 1.0 / (weights[0] + weights[1] + weights[2])
    ya = jnp.zeros((tm, D_MIX), _F32)
    for w, o in zip(weights, outs):
        c = jnp.where(lane < N_HEADS, w * inv, 0.0)
        c_hi = c.astype(_BF16)
        c_lo = (c - c_hi.astype(_F32)).astype(_BF16)
        spread = jnp.dot(jnp.concatenate([c_hi, c_lo], axis=1), e_ref[...],
                         preferred_element_type=_F32)
        ya = ya + spread * o
    ya = _rmsnorm(ya, gd_ref[...]).astype(_BF16)
    yb = _regroup_heads(yb_ref[...].astype(_F32), to_kernel_order=False)
    yb = _rmsnorm(yb, gw_ref[...]).astype(_BF16)
    x1 = (x_ref[...]
          + jnp.dot(ya, wout_ref[:D_MIX, :], preferred_element_type=_F32)
          + jnp.dot(yb, wout_ref[D_MIX:, :], preferred_element_type=_F32))
    h = _rmsnorm(x1, gf_ref[...]).astype(_BF16)
    for c0 in range(0, D_FF, FF_CHUNK):
        g = jnp.dot(h, wg_ref[:, c0:c0 + FF_CHUNK], preferred_element_type=_F32)
        u = jnp.dot(h, wu_ref[:, c0:c0 + FF_CHUNK], preferred_element_type=_F32)
        act_ref[:, c0:c0 + FF_CHUNK] = (g * jax.nn.sigmoid(g) * u).astype(_BF16)
    out = x1 + jnp.dot(act_ref[...], wd_ref[...], preferred_element_type=_F32)
    if final:
        out = _rmsnorm(out, gfin_ref[...])
    out_ref[...] = out


def _mix_ffn(x2, outs, stats, yb, spread, gd, gw, wout, gf, wg, wu, wd, gfin, layer, seq, final):
    n = x2.shape[0]
    tm = TOK_TILE_FFN
    tpb = seq // tm
    row = lambda width: pl.BlockSpec((tm, width), lambda i: (i, 0))
    whole = lambda a: pl.BlockSpec(a.shape, lambda i: (0, 0), pipeline_mode=pl.Buffered(1))
    per_pattern = lambda width: [row(width), _class_block(tm, 4, width, tpb),
                                 _class_block(tm, 16, width, tpb)]
    nat = lambda width: pltpu.VMEM((width // LANES, tm, LANES), _F32)
    cls = lambda width: pltpu.VMEM((width // LANES, 4, tm // 4, LANES), _F32)
    return pl.pallas_call(
        functools.partial(_mix_ffn_kernel, final=final),
        grid=(n // tm,),
        in_specs=[row(D_MODEL), *per_pattern(D_MIX), *per_pattern(LANES), row(D_MIX),
                  whole(spread), *[_layer_block(a, layer) for a in (gd, gw, wout, gf, wg, wu, wd)],
                  whole(gfin)],
        out_specs=row(D_MODEL),
        out_shape=jax.ShapeDtypeStruct((n, D_MODEL), _F32),
        scratch_shapes=[pltpu.VMEM((tm, D_FF), _BF16), nat(D_MIX), nat(D_MIX), cls(D_MIX),
                        nat(LANES), nat(LANES), cls(LANES)],
        compiler_params=pltpu.CompilerParams(dimension_semantics=("arbitrary",),
                                             vmem_limit_bytes=VMEM_LIMIT),
        name="mix_ffn",
    )(x2, *outs, *stats, yb, spread, gd, gw, wout, gf, wg, wu, wd, gfin)


def kernel(x, g_mix, w_in, g_out_dil, g_out_win, sink, w_out, g_ffn, w_gate, w_up, w_down, g_final):
    batch, seq, _ = x.shape
    depth = w_in.shape[0]
    slopes = _alibi_slopes(N_ALIBI)
    slopes_win = slopes[:N_HEADS][np.array(WIN_HEAD_ORDER)]
    slopes_dil = slopes[N_HEADS:]
    bias_dil = [jnp.asarray(_bias_table(w // (2 * d), d, slopes_dil)) for w, d in DIL_PATTERNS]
    bias_win = jnp.asarray(_bias_table(WIN_SIDE, 1, slopes_win))
    spread = np.zeros((2, LANES, D_MIX), np.float32)
    for h in range(N_HEADS):
        spread[:, h, h * HEAD_DIM:(h + 1) * HEAD_DIM] = 1.0
    spread = jnp.asarray(spread.reshape(2 * LANES, D_MIX), _BF16)

    w_in_k, w_out_k, w_gate_k, w_up_k, w_down_k = (
        w.astype(_BF16) for w in (w_in, w_out, w_gate, w_up, w_down))
    sink_k = sink * LOG2E
    vecs = lambda g: g.reshape(depth, 1, -1)

    x2 = x.reshape(batch * seq, D_MODEL)
    for i in range(depth):
        q1, k1, v1, q4, k4, v4, q16, k16, v16, qb, kb, vb = _inproj(
            x2, vecs(g_mix), w_in_k, i, batch, seq)
        as_class = lambda a: a.reshape(batch, 1, seq, a.shape[-1])
        qkv = ((as_class(q1), as_class(k1), as_class(v1)), (q4, k4, v4), (q16, k16, v16))
        outs, stats = [], []
        for (window, dil), (q, k, v), bias in zip(DIL_PATTERNS, qkv, bias_dil):
            o, st = _banded_attention(q, k, v, bias, window // (2 * dil))
            outs.append(o)
            stats.append(st)
        outs[0] = outs[0].reshape(batch * seq, D_MIX)
        stats[0] = stats[0].reshape(batch * seq, LANES)
        (yb,) = _banded_attention(as_class(qb), as_class(kb), as_class(vb), bias_win, WIN_SIDE,
                                  sink_k, i)
        x2 = _mix_ffn(x2, outs, stats, yb.reshape(batch * seq, D_MIX), spread, vecs(g_out_dil),
                      vecs(g_out_win), w_out_k, vecs(g_ffn), w_gate_k, w_up_k, w_down_k,
                      g_final.reshape(1, -1), i, seq, final=(i == depth - 1))
    return x2.reshape(batch, seq, D_MODEL)
```

```python
import functools
import math

import jax
import jax.numpy as jnp
import numpy as np
from jax import lax
from jax.experimental import pallas as pl
from jax.experimental.pallas import tpu as pltpu

D_MODEL = 1024
HEAD_DIM = 64
N_HEADS = 8
N_KV_WIN = 2
DIL_PATTERNS = ((128, 1), (512, 4), (2048, 16))
WIN_SIDE = 128
D_MIX = N_HEADS * HEAD_DIM
D_KV_WIN = N_KV_WIN * HEAD_DIM
D_FF = 2816
EPS = 1e-6
NEG = -1e30
N_ALIBI = 2 * N_HEADS
LOG2E = math.log2(math.e)

LANES = 128
FF_CHUNK = 256
Q_BLOCK = 128
Q_TILE = 2048
UNROLL_BLOCKS = 16
SCORE_LOOKAHEAD = 1
TOK_TILE_PROJ = 1024
TOK_TILE_MIX = 1024
TOK_TILE_FFN = 1024
VMEM_LIMIT = 56 * 1024 * 1024

WIN_HEAD_ORDER = (0, 4, 1, 5, 2, 6, 3, 7)

_F32 = jnp.float32
_BF16 = jnp.bfloat16


def _alibi_slopes(n):
    return np.array([2.0 ** (-8.0 * (i + 1) / n) for i in range(n)], dtype=np.float32)


def _bias_table(side, dil, slopes):
    kw = Q_BLOCK + 2 * side
    i = np.arange(Q_BLOCK)[:, None]
    j = np.arange(kw)[None, :]
    out = np.empty((3, len(slopes), Q_BLOCK, kw), np.float32)
    for v in range(3):
        rel = np.abs(j - i - side * v)
        dist = rel.astype(np.float32) * np.float32(dil)
        for h, sl in enumerate(slopes):
            out[v, h] = np.where(rel <= side, -(np.float32(sl) * dist) * np.float32(LOG2E),
                                 np.float32(NEG))
    return out.reshape(3, len(slopes) * Q_BLOCK, kw)


def _rmsnorm(x, g):
    return x * lax.rsqrt(jnp.mean(x * x, axis=-1, keepdims=True) + EPS) * g


def _regroup_heads(y, to_kernel_order):
    lo = lax.broadcasted_iota(jnp.int32, (y.shape[0], LANES), 1) < HEAD_DIM
    t = [y[:, j * LANES:(j + 1) * LANES] for j in range(D_MIX // LANES)]
    r = [pltpu.roll(x, HEAD_DIM, axis=1) for x in t]
    if to_kernel_order:
        out = [jnp.where(lo, t[0], r[2]), jnp.where(lo, r[0], t[2]),
               jnp.where(lo, t[1], r[3]), jnp.where(lo, r[1], t[3])]
    else:
        out = [jnp.where(lo, t[0], r[1]), jnp.where(lo, t[2], r[3]),
               jnp.where(lo, r[0], t[1]), jnp.where(lo, r[2], t[3])]
    return jnp.concatenate(out, axis=1)


def _inproj_kernel(x_ref, g_ref, w_ref, q1_ref, k1_ref, v1_ref, q4_ref, k4_ref, v4_ref,
                   q16_ref, k16_ref, v16_ref, qb_ref, kb_ref, vb_ref, nat_ref, cls_ref):
    tm = x_ref.shape[0]
    h = _rmsnorm(x_ref[...], g_ref[...]).astype(_BF16)
    scale = HEAD_DIM ** -0.5 * LOG2E

    def project(col, width, mul):
        p = jnp.dot(h, w_ref[:, col:col + width], preferred_element_type=_F32)
        return p if mul is None else p * mul

    dilated = ((q1_ref, q4_ref, q16_ref, scale), (k1_ref, k4_ref, k16_ref, None),
               (v1_ref, v4_ref, v16_ref, None))
    for n, (o1_ref, o4_ref, o16_ref, mul) in enumerate(dilated):
        p = project(n * D_MIX, D_MIX, mul)
        o1_ref[...] = p.astype(_BF16)
        for j in range(D_MIX // LANES):
            cols = slice(j * LANES, (j + 1) * LANES)
            nat_ref[n, j] = p[:, cols]
            for c in range(4):
                rows = nat_ref[n, j, pl.ds(c, tm // 4, stride=4), :]
                o4_ref[c, :, cols] = rows.astype(_BF16)
                cls_ref[n, j, c] = rows
            for c in range(4):
                for b in range(4):
                    rows = cls_ref[n, j, c, pl.ds(b, tm // 16, stride=4), :]
                    o16_ref[4 * b + c, :, cols] = rows.astype(_BF16)
    qb = project(3 * D_MIX, D_MIX, scale)
    qb_ref[...] = _regroup_heads(qb, to_kernel_order=True).astype(_BF16)
    kb_ref[...] = project(4 * D_MIX, D_KV_WIN, None).astype(_BF16)
    vb_ref[...] = project(4 * D_MIX + D_KV_WIN, D_KV_WIN, None).astype(_BF16)


def _class_block(tm, dil, width, tiles_per_batch):
    return pl.BlockSpec((None, dil, tm // dil, width),
                        lambda i: (i // tiles_per_batch, 0, i % tiles_per_batch, 0))


def _layer_block(stacked, layer):
    return pl.BlockSpec((None,) + stacked.shape[1:], lambda i: (layer, 0, 0),
                        pipeline_mode=pl.Buffered(1))


def _inproj(x2, g, w, layer, batch, seq):
    n = x2.shape[0]
    tm = TOK_TILE_PROJ
    tpb = seq // tm
    row = lambda width: pl.BlockSpec((tm, width), lambda i: (i, 0))
    flat = lambda width: jax.ShapeDtypeStruct((n, width), _BF16)
    cls = lambda dil: jax.ShapeDtypeStruct((batch, dil, seq // dil, D_MIX), _BF16)
    return pl.pallas_call(
        _inproj_kernel,
        grid=(n // tm,),
        in_specs=[row(D_MODEL),
                  _layer_block(g, layer), _layer_block(w, layer)],
        out_specs=([row(D_MIX)] * 3 + [_class_block(tm, 4, D_MIX, tpb)] * 3
                   + [_class_block(tm, 16, D_MIX, tpb)] * 3
                   + [row(D_MIX), row(D_KV_WIN), row(D_KV_WIN)]),
        out_shape=([flat(D_MIX)] * 3 + [cls(4)] * 3 + [cls(16)] * 3
                   + [flat(D_MIX), flat(D_KV_WIN), flat(D_KV_WIN)]),
        scratch_shapes=[pltpu.VMEM((3, D_MIX // LANES, tm, LANES), _F32),
                        pltpu.VMEM((3, D_MIX // LANES, 4, tm // 4, LANES), _F32)],
        compiler_params=pltpu.CompilerParams(dimension_semantics=("arbitrary",),
                                             vmem_limit_bytes=VMEM_LIMIT),
        name="inproj",
    )(x2, g, w)


def _block_window(t, tq, bi, side, length):
    kw = Q_BLOCK + 2 * side
    q0 = t * tq + bi * Q_BLOCK
    kstart = jnp.clip(q0 - side, 0, length - kw)
    var = lax.shift_right_logical(q0 - kstart, side.bit_length() - 1)
    return pl.multiple_of(bi * Q_BLOCK, Q_BLOCK), pl.multiple_of(kstart, side), var


def _split_heads(qp, lo):
    zero = jnp.zeros_like(qp)
    return jnp.concatenate([jnp.where(lo, qp, zero), jnp.where(lo, zero, qp)], axis=0)


def _with_ones(vp):
    return jnp.concatenate([vp, jnp.ones_like(vp)], axis=1)


def _banded_attn_kernel(*refs, side, length, tq, shared_kv, with_sink, layer):
    if with_sink:
        sink_ref, q_ref, k_ref, v_ref, bias_ref, o_ref = refs
    else:
        q_ref, k_ref, v_ref, bias_ref, o_ref, st_ref = refs
    t = pl.program_id(2)
    kw = Q_BLOCK + 2 * side
    n_pairs = N_HEADS // 2
    blocks_per_class = tq // Q_BLOCK
    n_iter = q_ref.shape[0] * blocks_per_class
    assert n_iter % UNROLL_BLOCKS == 0, (q_ref.shape, UNROLL_BLOCKS)
    lane = lax.broadcasted_iota(jnp.int32, (Q_BLOCK, LANES), 1)
    lo = lane < HEAD_DIM
    pair_cols = [slice(p * LANES, (p + 1) * LANES) for p in range(n_pairs)]
    kv_cols = [slice(0, LANES)] * n_pairs if shared_kv else pair_cols
    nt_dims = (((1,), (1,)), ((), ()))

    pair_groups = [[0, 1], [2, 3]] if shared_kv else [[n] for n in range(n_pairs)]

    def sink_rows(pairs):
        return jnp.concatenate([jnp.full((Q_BLOCK, LANES), sink_ref[layer, WIN_HEAD_ORDER[h]], _F32)
                                for h in range(2 * pairs[0], 2 * pairs[-1] + 2)], axis=0)

    def group_scores(ci, r0, kstart, var, pairs):
        rows = slice(2 * pairs[0] * Q_BLOCK, 2 * (pairs[-1] + 1) * Q_BLOCK)
        q = jnp.concatenate([_split_heads(q_ref[ci, pl.ds(r0, Q_BLOCK), pair_cols[n]], lo)
                             for n in pairs], axis=0)
        s = lax.dot_general(q, k_ref[ci, pl.ds(kstart, kw), kv_cols[pairs[0]]], nt_dims,
                            preferred_element_type=_F32)
        return s + bias_ref[var, rows, :]

    def group_outputs(ci, r0, kstart, pairs, s, st):
        n_rows = s.shape[0]
        cols = kv_cols[pairs[0]]
        m = jnp.broadcast_to(jnp.max(s, axis=-1, keepdims=True), (n_rows, LANES))
        if with_sink:
            m = jnp.maximum(m, sink_rows(pairs))
        p = jnp.concatenate([jnp.exp2(s[:, c:c + LANES] - m) for c in range(0, kw, LANES)],
                            axis=1).astype(_BF16)
        ol = jnp.dot(p, _with_ones(v_ref[ci, pl.ds(kstart, kw), cols]), preferred_element_type=_F32)
        o = ol[:, :LANES]
        l = ol[:, LANES:]
        den = l + jnp.exp2(sink_rows(pairs) - m) if with_sink else l
        for k, n in enumerate(pairs):
            top, mid, end = 2 * k * Q_BLOCK, (2 * k + 1) * Q_BLOCK, (2 * k + 2) * Q_BLOCK
            o_pair = jnp.where(lo, o[top:mid], o[mid:end])
            den_pair = jnp.where(lo, den[top:mid], den[mid:end])
            o_ref[ci, pl.ds(r0, Q_BLOCK), pair_cols[n]] = (o_pair * (1.0 / den_pair)).astype(_BF16)
            if not with_sink:
                for h, hrows in ((2 * n, slice(top, mid)), (2 * n + 1, slice(mid, end))):
                    st = jnp.where(lane == h, m[hrows], st)
                    st = jnp.where(lane == N_HEADS + h, l[hrows], st)
        return st

    def body(g, carry):
        units = [(j, pairs) for j in range(UNROLL_BLOCKS) for pairs in pair_groups]
        where, stats, pending = {}, {}, []
        for u in range(len(units) + SCORE_LOOKAHEAD):
            if u < len(units):
                j, pairs = units[u]
                if j not in where:
                    i = g * UNROLL_BLOCKS + j
                    where[j] = (i // blocks_per_class,) + _block_window(
                        t, tq, i % blocks_per_class, side, length)
                    stats[j] = jnp.zeros((Q_BLOCK, LANES), _F32)
                ci, r0, kstart, var = where[j]
                pending.append((j, pairs, group_scores(ci, r0, kstart, var, pairs)))
            if u >= SCORE_LOOKAHEAD:
                j, pairs, s = pending.pop(0)
                ci, r0, kstart, _ = where[j]
                stats[j] = group_outputs(ci, r0, kstart, pairs, s, stats[j])
                if not with_sink and pairs is pair_groups[-1]:
                    st_ref[ci, pl.ds(r0, Q_BLOCK), :] = stats[j]
        return carry

    lax.fori_loop(0, n_iter // UNROLL_BLOCKS, body, 0)


def _banded_attention(q, k, v, bias, side, sink=None, layer=0):
    batch, classes, length, _ = q.shape
    tq = min(Q_TILE, length)
    group = min(classes, Q_TILE // tq)
    kw = Q_BLOCK + 2 * side
    with_sink = sink is not None
    tile = lambda width: pl.BlockSpec((None, group, tq, width), lambda b, r, t: (b, r, t, 0))
    kv_full = pl.BlockSpec((None, group, length, k.shape[-1]), lambda b, r, t: (b, r, 0, 0))
    in_specs = [tile(D_MIX), kv_full, kv_full, pl.BlockSpec(bias.shape, lambda b, r, t: (0, 0, 0))]
    out_specs = [tile(D_MIX)]
    out_shape = [jax.ShapeDtypeStruct(q.shape, _BF16)]
    args = [q, k, v, bias]
    if with_sink:
        in_specs.insert(0, pl.BlockSpec(memory_space=pltpu.SMEM))
        args.insert(0, sink)
    else:
        out_specs.append(tile(LANES))
        out_shape.append(jax.ShapeDtypeStruct((batch, classes, length, LANES), _F32))
    return pl.pallas_call(
        functools.partial(_banded_attn_kernel, side=side, length=length, tq=tq,
                          shared_kv=k.shape[-1] == LANES, with_sink=with_sink, layer=layer),
        grid=(batch, classes // group, length // tq),
        in_specs=in_specs,
        out_specs=out_specs,
        out_shape=out_shape,
        compiler_params=pltpu.CompilerParams(dimension_semantics=("arbitrary",) * 3,
                                             vmem_limit_bytes=VMEM_LIMIT),
        name="win_attn" if with_sink else f"dil_attn_c{classes}",
    )(*args)


def _to_token_order(src4_ref, src16_ref, nat4_ref, nat16_ref, cls_ref):
    n_tiles, tm, _ = nat4_ref.shape
    for j in range(n_tiles):
        cols = slice(j * LANES, (j + 1) * LANES)
        for c in range(4):
            nat4_ref[j, pl.ds(c, tm // 4, stride=4), :] = src4_ref[c, :, cols].astype(_F32)
            for b in range(4):
                cls_ref[j, c, pl.ds(b, tm // 16, stride=4), :] = (
                    src16_ref[4 * b + c, :, cols].astype(_F32))
        for c in range(4):
            nat16_ref[j, pl.ds(c, tm // 4, stride=4), :] = cls_ref[j, c]


def _lane_tiles(ref):
    return jnp.concatenate([ref[j] for j in range(ref.shape[0])], axis=-1)


def _mix_kernel(x_ref, o1_ref, o4_ref, o16_ref, s1_ref, s4_ref, s16_ref, yb_ref, e_ref,
                gd_ref, gw_ref, wout_ref, out_ref,
                on4_ref, on16_ref, ocls_ref, sn4_ref, sn16_ref, scls_ref):
    tm = x_ref.shape[0]
    _to_token_order(o4_ref, o16_ref, on4_ref, on16_ref, ocls_ref)
    _to_token_order(s4_ref, s16_ref, sn4_ref, sn16_ref, scls_ref)
    lane = lax.broadcasted_iota(jnp.int32, (tm, LANES), 1)
    stats = [s1_ref[...], sn4_ref[0], sn16_ref[0]]
    outs = [o1_ref[...].astype(_F32), _lane_tiles(on4_ref), _lane_tiles(on16_ref)]
    big = jnp.maximum(jnp.maximum(stats[0], stats[1]), stats[2])
    weights = [jnp.exp2(st - big) * pltpu.roll(st, LANES - N_HEADS, axis=1) for st in stats]
    inv = 1.0 / (weights[0] + weights[1] + weights[2])
    ya = jnp.zeros((tm, D_MIX), _F32)
    for w, o in zip(weights, outs):
        c = jnp.where(lane < N_HEADS, w * inv, 0.0)
        c_hi = c.astype(_BF16)
        c_lo = (c - c_hi.astype(_F32)).astype(_BF16)
        spread = jnp.dot(jnp.concatenate([c_hi, c_lo], axis=1), e_ref[...],
                         preferred_element_type=_F32)
        ya = ya + spread * o
    ya = _rmsnorm(ya, gd_ref[...]).astype(_BF16)
    yb = _regroup_heads(yb_ref[...].astype(_F32), to_kernel_order=False)
    yb = _rmsnorm(yb, gw_ref[...]).astype(_BF16)
    out_ref[...] = (x_ref[...]
                    + jnp.dot(ya, wout_ref[:D_MIX, :], preferred_element_type=_F32)
                    + jnp.dot(yb, wout_ref[D_MIX:, :], preferred_element_type=_F32))


def _ffn_kernel(x_ref, gf_ref, wg_ref, wu_ref, wd_ref, gfin_ref, out_ref, act_ref, *, final):
    x1 = x_ref[...]
    h = _rmsnorm(x1, gf_ref[...]).astype(_BF16)
    for c0 in range(0, D_FF, FF_CHUNK):
        g = jnp.dot(h, wg_ref[:, c0:c0 + FF_CHUNK], preferred_element_type=_F32)
        u = jnp.dot(h, wu_ref[:, c0:c0 + FF_CHUNK], preferred_element_type=_F32)
        act_ref[:, c0:c0 + FF_CHUNK] = (g * jax.nn.sigmoid(g) * u).astype(_BF16)
    out = x1 + jnp.dot(act_ref[...], wd_ref[...], preferred_element_type=_F32)
    if final:
        out = _rmsnorm(out, gfin_ref[...])
    out_ref[...] = out


def _mix_ffn(x2, outs, stats, yb, spread, gd, gw, wout, gf, wg, wu, wd, gfin, layer, seq, final):
    n = x2.shape[0]
    tm = TOK_TILE_MIX
    tpb = seq // tm
    row = lambda width: pl.BlockSpec((tm, width), lambda i: (i, 0))
    whole = lambda a: pl.BlockSpec(a.shape, lambda i: (0, 0), pipeline_mode=pl.Buffered(1))
    per_pattern = lambda width: [row(width), _class_block(tm, 4, width, tpb),
                                 _class_block(tm, 16, width, tpb)]
    nat = lambda width: pltpu.VMEM((width // LANES, tm, LANES), _F32)
    cls = lambda width: pltpu.VMEM((width // LANES, 4, tm // 4, LANES), _F32)
    params = pltpu.CompilerParams(dimension_semantics=("arbitrary",), vmem_limit_bytes=VMEM_LIMIT)
    x1 = pl.pallas_call(
        _mix_kernel,
        grid=(n // tm,),
        in_specs=[row(D_MODEL), *per_pattern(D_MIX), *per_pattern(LANES), row(D_MIX),
                  whole(spread), *[_layer_block(a, layer) for a in (gd, gw, wout)]],
        out_specs=row(D_MODEL),
        out_shape=jax.ShapeDtypeStruct((n, D_MODEL), _F32),
        scratch_shapes=[nat(D_MIX), nat(D_MIX), cls(D_MIX), nat(LANES), nat(LANES), cls(LANES)],
        compiler_params=params,
        name="mix",
    )(x2, *outs, *stats, yb, spread, gd, gw, wout)
    tf = TOK_TILE_FFN
    ffn_row = pl.BlockSpec((tf, D_MODEL), lambda i: (i, 0))
    return pl.pallas_call(
        functools.partial(_ffn_kernel, final=final),
        grid=(n // tf,),
        in_specs=[ffn_row, *[_layer_block(a, layer) for a in (gf, wg, wu, wd)], whole(gfin)],
        out_specs=ffn_row,
        out_shape=jax.ShapeDtypeStruct((n, D_MODEL), _F32),
        scratch_shapes=[pltpu.VMEM((tf, D_FF), _BF16)],
        compiler_params=params,
        name="ffn",
    )(x1, gf, wg, wu, wd, gfin)


def kernel(x, g_mix, w_in, g_out_dil, g_out_win, sink, w_out, g_ffn, w_gate, w_up, w_down, g_final):
    batch, seq, _ = x.shape
    depth = w_in.shape[0]
    slopes = _alibi_slopes(N_ALIBI)
    slopes_win = slopes[:N_HEADS][np.array(WIN_HEAD_ORDER)]
    slopes_dil = slopes[N_HEADS:]
    bias_dil = [jnp.asarray(_bias_table(w // (2 * d), d, slopes_dil)) for w, d in DIL_PATTERNS]
    bias_win = jnp.asarray(_bias_table(WIN_SIDE, 1, slopes_win))
    spread = np.zeros((2, LANES, D_MIX), np.float32)
    for h in range(N_HEADS):
        spread[:, h, h * HEAD_DIM:(h + 1) * HEAD_DIM] = 1.0
    spread = jnp.asarray(spread.reshape(2 * LANES, D_MIX), _BF16)

    w_in_k, w_out_k, w_gate_k, w_up_k, w_down_k = (
        w.astype(_BF16) for w in (w_in, w_out, w_gate, w_up, w_down))
    sink_k = sink * LOG2E
    vecs = lambda g: g.reshape(depth, 1, -1)

    x2 = x.reshape(batch * seq, D_MODEL)
    for i in range(depth):
        q1, k1, v1, q4, k4, v4, q16, k16, v16, qb, kb, vb = _inproj(
            x2, vecs(g_mix), w_in_k, i, batch, seq)
        as_class = lambda a: a.reshape(batch, 1, seq, a.shape[-1])
        qkv = ((as_class(q1), as_class(k1), as_class(v1)), (q4, k4, v4), (q16, k16, v16))
        outs, stats = [], []
        for (window, dil), (q, k, v), bias in zip(DIL_PATTERNS, qkv, bias_dil):
            o, st = _banded_attention(q, k, v, bias, window // (2 * dil))
            outs.append(o)
            stats.append(st)
        outs[0] = outs[0].reshape(batch * seq, D_MIX)
        stats[0] = stats[0].reshape(batch * seq, LANES)
        (yb,) = _banded_attention(as_class(qb), as_class(kb), as_class(vb), bias_win, WIN_SIDE,
                                  sink_k, i)
        x2 = _mix_ffn(x2, outs, stats, yb.reshape(batch * seq, D_MIX), spread, vecs(g_out_dil),
                      vecs(g_out_win), w_out_k, vecs(g_ffn), w_gate_k, w_up_k, w_down_k,
                      g_final.reshape(1, -1), i, seq, final=(i == depth - 1))
    return x2.reshape(batch, seq, D_MODEL)
```
